```python
import jax, jax.numpy as jnp
from jax import lax
import numpy as np

D_MODEL = 1024
BATCH = 8
SEQ = 4096
DEPTH = 1
DEC_BATCH = 8
DEC_SEQ = 32
PAST_LEN = 4096

CHUNK = 64
MIX_WIDTH = D_MODEL
WIDTH_A = MIX_WIDTH // 2
WIDTH_B = MIX_WIDTH - WIDTH_A
IN_COLS = 3 * WIDTH_A + 2 * WIDTH_B
CONV_A = 3
CONV_B = 31
N_KEYS = 128
N_EXPERTS = N_KEYS * N_KEYS
PEER_HEADS = 8
PEER_TOPK = 16
D_KEY = 128
PEER_BLOCK = 128
EPS = 1e-6

kernel_name = 'hymba_style_conv_hybrid_peer_stream_step'


def rmsnorm(x, g):
    xf = x.astype(jnp.float32)
    y = xf * lax.rsqrt(jnp.mean(xf * xf, axis=-1, keepdims=True) + EPS)
    return (y * g.astype(jnp.float32)).astype(x.dtype)


def layernorm(x, g, b):
    xf = x.astype(jnp.float32)
    mu = jnp.mean(xf, axis=-1, keepdims=True)
    xc = xf - mu
    y = xc * lax.rsqrt(jnp.mean(xc * xc, axis=-1, keepdims=True) + EPS)
    return (y * g.astype(jnp.float32) + b.astype(jnp.float32)).astype(x.dtype)


def causal_dwconv(hist, x, w):
    k = w.shape[0]
    xp = jnp.concatenate([hist.astype(x.dtype), x], axis=1)
    y = lax.conv_general_dilated(xp, w[:, None, :].astype(x.dtype), window_strides=(1,), padding='VALID',
                                 dimension_numbers=('NWC', 'WIO', 'NWC'), feature_group_count=x.shape[-1])
    return y, xp[:, -(k - 1):]


def token_mixers(xn, hist_a, hist_b, w_in, conv_a_w, conv_b_w, conv_b_b, conv_norm_g, conv_norm_b,
                 out_norm_a_g, out_norm_b_g, w_out):
    proj = jnp.einsum('bld,de->ble', xn, w_in)
    h_a, c_a, b_a, v_b, g_b = jnp.split(
        proj, [WIDTH_A, 2 * WIDTH_A, 3 * WIDTH_A, 3 * WIDTH_A + WIDTH_B], axis=-1)
    conv_out_a, new_a = causal_dwconv(hist_a, c_a * h_a, conv_a_w)
    y_a = rmsnorm(b_a * conv_out_a, out_norm_a_g)
    glu = v_b * jax.nn.sigmoid(g_b)
    conv_out_b, new_b = causal_dwconv(hist_b, glu, conv_b_w)
    z = jax.nn.silu(layernorm(conv_out_b + conv_b_b.astype(glu.dtype), conv_norm_g, conv_norm_b))
    y_b = rmsnorm(z, out_norm_b_g)
    y = jnp.einsum('ble,ed->bld', jnp.concatenate([y_a, y_b], axis=-1), w_out)
    return y, new_a, new_b


def peer_tokens(xt, w_q, sub_keys, u_experts, v_experts):
    q = jnp.einsum('td,de->te', xt, w_q).reshape(xt.shape[0], PEER_HEADS, 2, D_KEY)
    s = jnp.einsum('thpc,hpkc->thpk', q, sub_keys).astype(jnp.float32)
    s1, i1 = lax.top_k(s[:, :, 0], PEER_TOPK)
    s2, i2 = lax.top_k(s[:, :, 1], PEER_TOPK)
    cand = (s1[..., :, None] + s2[..., None, :]).reshape(xt.shape[0], PEER_HEADS, PEER_TOPK * PEER_TOPK)
    top_s, j = lax.top_k(cand, PEER_TOPK)
    e = (jnp.take_along_axis(i1, j // PEER_TOPK, axis=-1) * N_KEYS
         + jnp.take_along_axis(i2, j % PEER_TOPK, axis=-1))
    gate = jax.nn.softmax(top_s, axis=-1).astype(xt.dtype)
    u = jnp.take(u_experts, e, axis=0)
    v = jnp.take(v_experts, e, axis=0)
    act = jax.nn.gelu(jnp.einsum('td,thkd->thk', xt, u), approximate=False)
    return jnp.einsum('thk,thkd->td', gate * act, v)


def peer(xn, w_q, sub_keys, u_experts, v_experts):
    n, l, d = xn.shape
    t = n * l
    xt = xn.reshape(t, d)
    fn = lambda blk: peer_tokens(blk, w_q, sub_keys, u_experts, v_experts)
    if t % PEER_BLOCK == 0 and t > PEER_BLOCK:
        out = lax.map(fn, xt.reshape(t // PEER_BLOCK, PEER_BLOCK, d)).reshape(t, d)
    else:
        out = fn(xt)
    return out.reshape(n, l, d)


def trunk(x, hist_a, hist_b, norm_mix_g, w_in, conv_a_w, conv_b_w, conv_b_b, conv_norm_g, conv_norm_b,
          out_norm_a_g, out_norm_b_g, w_out, norm_ffn_g, w_q, sub_keys, u_experts, v_experts, final_norm_g):
    new_a, new_b = [], []
    for layer in range(DEPTH):
        y, ha, hb = token_mixers(rmsnorm(x, norm_mix_g[layer]), hist_a[layer], hist_b[layer], w_in[layer],
                                 conv_a_w[layer], conv_b_w[layer], conv_b_b[layer], conv_norm_g[layer],
                                 conv_norm_b[layer], out_norm_a_g[layer], out_norm_b_g[layer], w_out[layer])
        h = x + y
        x = h + peer(rmsnorm(h, norm_ffn_g[layer]), w_q[layer], sub_keys[layer], u_experts[layer], v_experts[layer])
        new_a.append(ha)
        new_b.append(hb)
    return rmsnorm(x, final_norm_g), jnp.stack(new_a), jnp.stack(new_b)


def setup_inputs(seed: int = 0) -> dict:
    key = jax.random.key(seed)
    ks = jax.random.split(key, 20)
    f32 = jnp.float32

    def nrm(k, shape, scale):
        return jax.random.normal(k, shape, f32) * scale

    def gain(k, shape):
        return 1.0 + 0.05 * jax.random.normal(k, shape, f32)

    return {
        'x_prompt': nrm(ks[0], (BATCH, SEQ, D_MODEL), 1.0),
        'x_sample': nrm(ks[1], (DEC_BATCH, DEC_SEQ, D_MODEL), 1.0),
        'cache_conv_a': nrm(ks[2], (DEPTH, DEC_BATCH, CONV_A - 1, WIDTH_A), 1.0),
        'cache_conv_b': nrm(ks[3], (DEPTH, DEC_BATCH, CONV_B - 1, WIDTH_B), 1.0),
        'norm_mix_g': gain(ks[4], (DEPTH, D_MODEL)),
        'w_in': nrm(ks[5], (DEPTH, D_MODEL, IN_COLS), D_MODEL ** -0.5),
        'conv_a_w': nrm(ks[6], (DEPTH, CONV_A, WIDTH_A), CONV_A ** -0.5),
        'conv_b_w': nrm(ks[7], (DEPTH, CONV_B, WIDTH_B), CONV_B ** -0.5),
        'conv_b_b': nrm(ks[8], (DEPTH, WIDTH_B), 0.02),
        'conv_norm_g': gain(ks[9], (DEPTH, WIDTH_B)),
        'conv_norm_b': nrm(ks[10], (DEPTH, WIDTH_B), 0.02),
        'out_norm_a_g': gain(ks[11], (DEPTH, WIDTH_A)),
        'out_norm_b_g': gain(ks[12], (DEPTH, WIDTH_B)),
        'w_out': nrm(ks[13], (DEPTH, MIX_WIDTH, D_MODEL), MIX_WIDTH ** -0.5),
        'norm_ffn_g': gain(ks[14], (DEPTH, D_MODEL)),
        'w_q': nrm(ks[15], (DEPTH, D_MODEL, PEER_HEADS * 2 * D_KEY), D_MODEL ** -0.5),
        'sub_keys': nrm(ks[16], (DEPTH, PEER_HEADS, 2, N_KEYS, D_KEY), D_KEY ** -0.5),
        'u_experts': nrm(ks[17], (DEPTH, N_EXPERTS, D_MODEL), D_MODEL ** -0.5),
        'v_experts': nrm(ks[18], (DEPTH, N_EXPERTS, D_MODEL), PEER_HEADS ** -0.5),
        'final_norm_g': gain(ks[19], (D_MODEL,)),
    }


def reference(x_prompt, x_sample, cache_conv_a, cache_conv_b, norm_mix_g, w_in, conv_a_w, conv_b_w, conv_b_b,
              conv_norm_g, conv_norm_b, out_norm_a_g, out_norm_b_g, w_out, norm_ffn_g, w_q, sub_keys,
              u_experts, v_experts, final_norm_g):
    zeros_a = jnp.zeros((DEPTH, x_prompt.shape[0], CONV_A - 1, WIDTH_A), x_prompt.dtype)
    zeros_b = jnp.zeros((DEPTH, x_prompt.shape[0], CONV_B - 1, WIDTH_B), x_prompt.dtype)
    y_prompt, conv_a_prompt, conv_b_prompt = trunk(
        x_prompt, zeros_a, zeros_b, norm_mix_g, w_in, conv_a_w, conv_b_w, conv_b_b, conv_norm_g, conv_norm_b,
        out_norm_a_g, out_norm_b_g, w_out, norm_ffn_g, w_q, sub_keys, u_experts, v_experts, final_norm_g)
    y_sample, conv_a_sample, conv_b_sample = trunk(
        x_sample, cache_conv_a, cache_conv_b, norm_mix_g, w_in, conv_a_w, conv_b_w, conv_b_b, conv_norm_g,
        conv_norm_b, out_norm_a_g, out_norm_b_g, w_out, norm_ffn_g, w_q, sub_keys, u_experts, v_experts,
        final_norm_g)
    return (y_prompt, y_sample, conv_a_prompt, conv_b_prompt, conv_a_sample, conv_b_sample)
```

```python
import functools

import numpy as np
import jax
import jax.numpy as jnp
from jax import lax
from jax.experimental import pallas as pl
from jax.experimental.pallas import tpu as pltpu
from jax.experimental.pallas import tpu_sc as plsc

F32 = jnp.float32
I32 = jnp.int32
HIGHEST = lax.Precision.HIGHEST
EPS = 1e-6

LANES = 128
SUBLANES = 8
VMEM_LIMIT_BYTES = 56 * 1024 * 1024
SC_CORES = 2
SC_SUBCORES = 16
SC_LANES = 16
SC_WORKERS = SC_CORES * SC_SUBCORES

PEER_TOPK = 16


def _rms(x, g):
    return x * lax.rsqrt(jnp.mean(x * x, axis=-1, keepdims=True) + EPS) * g


def _const_spec(shape):
    zeros = (0,) * len(shape)
    return pl.BlockSpec(shape, lambda *_: zeros, pipeline_mode=pl.Buffered(1))


CONV_ROW_BLOCK = 64


def _mixer_body(x_ref, ha_ref, hb_ref, gmix_ref, win_ref, caw_ref, cbw_ref, cbb_ref, cng_ref, cnb_ref,
                ga_ref, gb_ref, wout_ref, h_ref, na_ref, nb_ref, xa_s, xb_s, cb_s,
                *, tm, wa, wb, ka, kb, pad_a, pad_b):
    i = pl.program_id(1)

    @pl.when(i == 0)
    def _():
        xa_s[0:pad_a] = ha_ref[...]
        xb_s[0:pad_b] = hb_ref[...]

    x = x_ref[...]
    xn = _rms(x, gmix_ref[...])
    proj = jnp.dot(xn, win_ref[...], precision=HIGHEST, preferred_element_type=F32)
    h_a = proj[:, 0:wa]
    c_a = proj[:, wa:2 * wa]
    b_a = proj[:, 2 * wa:3 * wa]
    v_b = proj[:, 3 * wa:3 * wa + wb]
    g_b = proj[:, 3 * wa + wb:3 * wa + 2 * wb]

    xa_s[pad_a:pad_a + tm] = c_a * h_a
    conv_a = caw_ref[0:1, :] * xa_s[pad_a - (ka - 1):pad_a - (ka - 1) + tm]
    for k in range(1, ka):
        lo = pad_a - (ka - 1) + k
        conv_a = conv_a + caw_ref[k:k + 1, :] * xa_s[lo:lo + tm]
    y_a = _rms(b_a * conv_a, ga_ref[...])

    xb_s[pad_b:pad_b + tm] = v_b * jax.nn.sigmoid(g_b)
    rb = min(tm, CONV_ROW_BLOCK)
    for r0 in range(0, tm, rb):
        base = r0 + pad_b - (kb - 1)
        acc = cbw_ref[0:1, :] * xb_s[base:base + rb]
        for k in range(1, kb):
            acc = acc + cbw_ref[k:k + 1, :] * xb_s[base + k:base + k + rb]
        cb_s[r0:r0 + rb] = acc + cbb_ref[...]
    cb = cb_s[...]
    mu = jnp.mean(cb, axis=-1, keepdims=True)
    xc = cb - mu
    ln = xc * lax.rsqrt(jnp.mean(xc * xc, axis=-1, keepdims=True) + EPS) * cng_ref[...] + cnb_ref[...]
    z = ln * jax.nn.sigmoid(ln)
    y_b = _rms(z, gb_ref[...])

    y = (jnp.dot(y_a, wout_ref[0:wa, :], precision=HIGHEST, preferred_element_type=F32)
         + jnp.dot(y_b, wout_ref[wa:wa + wb, :], precision=HIGHEST, preferred_element_type=F32))
    h_ref[...] = x + y

    xa_s[0:pad_a] = xa_s[tm:tm + pad_a]
    xb_s[0:pad_b] = xb_s[tm:tm + pad_b]

    @pl.when(i == pl.num_programs(1) - 1)
    def _():
        na_ref[...] = xa_s[0:pad_a]
        nb_ref[...] = xb_s[0:pad_b]


def _mixer(x, hist_a, hist_b, gmix, w_in, caw, cbw, cbb, cng, cnb, ga, gb, w_out):
    n, l, d = x.shape
    ka, wa = caw.shape
    kb, wb = cbw.shape
    pad_a = SUBLANES
    pad_b = -(-(kb - 1) // SUBLANES) * SUBLANES
    tm = min(l, 512)
    assert l % tm == 0 and tm % min(tm, CONV_ROW_BLOCK) == 0
    assert tm >= pad_b and tm % SUBLANES == 0 and ka - 1 <= pad_a
    ha = jnp.pad(hist_a, ((0, 0), (pad_a - (ka - 1), 0), (0, 0)))
    hb = jnp.pad(hist_b, ((0, 0), (pad_b - (kb - 1), 0), (0, 0)))
    row = lambda v: v.reshape(1, -1)
    body = functools.partial(_mixer_body, tm=tm, wa=wa, wb=wb, ka=ka, kb=kb, pad_a=pad_a, pad_b=pad_b)
    h, na, nb = pl.pallas_call(
        body,
        grid=(n, l // tm),
        in_specs=[
            pl.BlockSpec((None, tm, d), lambda b, i: (b, i, 0)),
            pl.BlockSpec((None, pad_a, wa), lambda b, i: (b, 0, 0)),
            pl.BlockSpec((None, pad_b, wb), lambda b, i: (b, 0, 0)),
            _const_spec((1, d)),
            _const_spec(w_in.shape),
            _const_spec(caw.shape),
            _const_spec(cbw.shape),
            _const_spec((1, wb)),
            _const_spec((1, wb)),
            _const_spec((1, wb)),
            _const_spec((1, wa)),
            _const_spec((1, wb)),
            _const_spec(w_out.shape),
        ],
        out_specs=[
            pl.BlockSpec((None, tm, d), lambda b, i: (b, i, 0)),
            pl.BlockSpec((None, pad_a, wa), lambda b, i: (b, 0, 0)),
            pl.BlockSpec((None, pad_b, wb), lambda b, i: (b, 0, 0)),
        ],
        out_shape=[
            jax.ShapeDtypeStruct((n, l, d), F32),
            jax.ShapeDtypeStruct((n, pad_a, wa), F32),
            jax.ShapeDtypeStruct((n, pad_b, wb), F32),
        ],
        scratch_shapes=[
            pltpu.VMEM((pad_a + tm, wa), F32),
            pltpu.VMEM((pad_b + tm, wb), F32),
            pltpu.VMEM((tm, wb), F32),
        ],
        compiler_params=pltpu.CompilerParams(
            dimension_semantics=("parallel", "arbitrary"), vmem_limit_bytes=VMEM_LIMIT_BYTES),
        name="mixer",
    )(x, ha, hb, row(gmix), w_in, caw, cbw, row(cbb), row(cng), row(cnb), row(ga), row(gb), w_out)
    return h, na[:, pad_a - (ka - 1):], nb[:, pad_b - (kb - 1):]


def _top_rows(s, k):
    n, lanes = s.shape
    rows = lax.broadcasted_iota(I32, s.shape, 0)
    slot = lax.broadcasted_iota(I32, (k, lanes), 0)
    vals = jnp.zeros((k, lanes), F32)
    ids = jnp.zeros((k, lanes), I32)
    for it in range(k):
        m = jnp.max(s, axis=0, keepdims=True)
        am = jnp.min(jnp.where(s == m, rows, n), axis=0, keepdims=True)
        vals = jnp.where(slot == it, m, vals)
        ids = jnp.where(slot == it, am, ids)
        s = jnp.where(rows == am, -jnp.inf, s)
    return vals, ids


def _combine(v1, i1, v2, i2, n_keys):
    k, lanes = v1.shape
    half = k // 2
    vals, pos, eid = [], [], []

    def add(a_lo, a_n, b_lo, b_n):
        vals.append(v1[a_lo:a_lo + a_n] + v2[b_lo:b_lo + b_n])
        eid.append(i1[a_lo:a_lo + a_n] * n_keys + i2[b_lo:b_lo + b_n])
        r = lax.broadcasted_iota(I32, (max(a_n, b_n), lanes), 0)
        pos.append((a_lo + r) * k + b_lo if a_n > 1 else a_lo * k + b_lo + r)

    add(0, 1, 0, k)
    for a in range(1, half):
        add(a, 1, 0, half)
    add(half, k - half, 0, 1)
    cand = jnp.concatenate(vals, axis=0)
    pos = jnp.concatenate(pos, axis=0)
    eid = jnp.concatenate(eid, axis=0)

    slot = lax.broadcasted_iota(I32, (k, lanes), 0)
    top_s = jnp.zeros((k, lanes), F32)
    top_e = jnp.zeros((k, lanes), I32)
    for it in range(k):
        m = jnp.max(cand, axis=0, keepdims=True)
        sel = jnp.min(jnp.where(cand == m, pos, k * k), axis=0, keepdims=True)
        hit = pos == sel
        e = jnp.max(jnp.where(hit, eid, -1), axis=0, keepdims=True)
        top_s = jnp.where(slot == it, m, top_s)
        top_e = jnp.where(slot == it, e, top_e)
        cand = jnp.where(hit, -jnp.inf, cand)
    return top_s, top_e


def _route_body(h_ref, g_ref, wq_ref, sk_ref, xn_ref, idx_ref, gate_ref, q_s, st_s, et_s, gt_s,
                *, tm, heads, n_keys, d_key):
    hn = _rms(h_ref[...], g_ref[...])
    xn_ref[...] = hn
    q_s[...] = jnp.dot(hn, wq_ref[...], precision=HIGHEST, preferred_element_type=F32)
    for hp in range(2 * heads):
        st_s[hp] = lax.dot_general(sk_ref[hp], q_s[:, hp * d_key:(hp + 1) * d_key],
                                   (((1,), (1,)), ((), ())), precision=HIGHEST, preferred_element_type=F32)
    n_chunk = tm // LANES

    def per_head_chunk(j, carry):
        hd = j // n_chunk
        col = pl.multiple_of((j % n_chunk) * LANES, LANES)
        v1, i1 = _top_rows(st_s[2 * hd, :, pl.ds(col, LANES)], PEER_TOPK)
        v2, i2 = _top_rows(st_s[2 * hd + 1, :, pl.ds(col, LANES)], PEER_TOPK)
        top_s, top_e = _combine(v1, i1, v2, i2, n_keys)
        p = jnp.exp(top_s - top_s[0:1])
        gate = p / jnp.sum(p, axis=0, keepdims=True)
        row = pl.multiple_of(hd * PEER_TOPK, PEER_TOPK)
        et_s[pl.ds(row, PEER_TOPK), pl.ds(col, LANES)] = top_e
        gt_s[pl.ds(row, PEER_TOPK), pl.ds(col, LANES)] = gate
        return carry

    lax.fori_loop(0, heads * n_chunk, per_head_chunk, 0)
    idx_ref[...] = et_s[...].T
    gate_ref[...] = gt_s[...].T


def _route(h2, g, w_q, sub_keys):
    t, d = h2.shape
    heads, _, n_keys, d_key = sub_keys.shape
    hk = heads * PEER_TOPK
    tm = min(t, 512)
    assert t % tm == 0 and tm % LANES == 0 and d_key == LANES and n_keys % SUBLANES == 0
    sk = sub_keys.reshape(2 * heads, n_keys, d_key)
    body = functools.partial(_route_body, tm=tm, heads=heads, n_keys=n_keys, d_key=d_key)
    return pl.pallas_call(
        body,
        grid=(t // tm,),
        in_specs=[
            pl.BlockSpec((tm, d), lambda i: (i, 0)),
            _const_spec((1, d)),
            _const_spec(w_q.shape),
            _const_spec(sk.shape),
        ],
        out_specs=[
            pl.BlockSpec((tm, d), lambda i: (i, 0)),
            pl.BlockSpec((tm, hk), lambda i: (i, 0)),
            pl.BlockSpec((tm, hk), lambda i: (i, 0)),
        ],
        out_shape=[
            jax.ShapeDtypeStruct((t, d), F32),
            jax.ShapeDtypeStruct((t, hk), I32),
            jax.ShapeDtypeStruct((t, hk), F32),
        ],
        scratch_shapes=[
            pltpu.VMEM((tm, 2 * heads * d_key), F32),
            pltpu.VMEM((2 * heads, n_keys, tm), F32),
            pltpu.VMEM((hk, tm), I32),
            pltpu.VMEM((hk, tm), F32),
        ],
        compiler_params=pltpu.CompilerParams(
            dimension_semantics=("parallel",), vmem_limit_bytes=VMEM_LIMIT_BYTES),
        name="route",
    )(h2, g.reshape(1, d), w_q, sk)


GATHER_ROWS = 16
GATHER_BUFFERS = 4
LANE_STEPS = 4


def _sc_mesh():
    return plsc.VectorSubcoreMesh(core_axis_name="c", subcore_axis_name="s")


def _sc_geometry(t, hk):
    per_worker = t // SC_WORKERS
    group = min(per_worker, 8)
    chunks = hk // GATHER_ROWS
    assert t % SC_WORKERS == 0 and per_worker % group == 0 and group % SUBLANES == 0
    assert hk % GATHER_ROWS == 0 and chunks % GATHER_BUFFERS == 0 and GATHER_ROWS == SC_LANES
    return per_worker, group, chunks


def _gather_ring(table_hbm, idx_v, ring, sems, group, chunks, compute):
    def copy(g, c, slot):
        rows = idx_v[g, pl.ds(c * GATHER_ROWS, GATHER_ROWS)]
        return pltpu.make_async_copy(table_hbm.at[rows], ring.at[slot], sems.at[slot])

    for c in range(GATHER_BUFFERS):
        copy(0, c, c).start()

    @pl.loop(0, group)
    def _(g):
        for c in range(chunks):
            slot = c % GATHER_BUFFERS
            copy(g, c, slot).wait()
            compute(g, c, slot)
            nxt = c + GATHER_BUFFERS
            if nxt < chunks:
                copy(g, nxt, slot).start()
            else:
                @pl.when(g + 1 < group)
                def _():
                    copy(g + 1, nxt - chunks, slot).start()


def _peer_dots(xn, idx, table):
    t, d = xn.shape
    hk = idx.shape[1]
    per_worker, group, chunks = _sc_geometry(t, hk)
    steps = d // SC_LANES

    def body(xn_hbm, idx_hbm, table_hbm, out_hbm, idx_v, x_v, o_v, ring, sems):
        wid = lax.axis_index("c") * SC_SUBCORES + lax.axis_index("s")
        lane = lax.iota(I32, SC_LANES)

        def compute(g, c, slot):
            def col_block(cb, accs):
                for u in range(LANE_STEPS):
                    col = (cb * LANE_STEPS + u) * SC_LANES
                    xc = x_v[g, pl.ds(col, SC_LANES)]
                    accs = tuple(a + ring[slot, r, pl.ds(col, SC_LANES)] * xc for r, a in enumerate(accs))
                return accs

            zero = jnp.zeros((SC_LANES,), F32)
            accs = lax.fori_loop(0, steps // LANE_STEPS, col_block, (zero,) * GATHER_ROWS)
            tot = zero
            for r in range(GATHER_ROWS):
                tot = jnp.where(lane == r, jnp.sum(accs[r]), tot)
            o_v[g, pl.ds(c * GATHER_ROWS, GATHER_ROWS)] = tot

        @pl.loop(0, per_worker // group)
        def _(grp):
            tok0 = pl.multiple_of(wid * per_worker + grp * group, SUBLANES)
            pltpu.sync_copy(idx_hbm.at[pl.ds(tok0, group)], idx_v)
            pltpu.sync_copy(xn_hbm.at[pl.ds(tok0, group)], x_v)
            _gather_ring(table_hbm, idx_v, ring, sems, group, chunks, compute)
            pltpu.sync_copy(o_v, out_hbm.at[pl.ds(tok0, group)])

    return pl.kernel(
        body,
        out_type=jax.ShapeDtypeStruct((t, hk), F32),
        mesh=_sc_mesh(),
        scratch_types=[
            pltpu.VMEM((group, hk), I32),
            pltpu.VMEM((group, d), F32),
            pltpu.VMEM((group, hk), F32),
            pltpu.VMEM((GATHER_BUFFERS, GATHER_ROWS, d), F32),
            pltpu.SemaphoreType.DMA((GATHER_BUFFERS,)),
        ],
        compiler_params=pltpu.CompilerParams(needs_layout_passes=False),
        name="peer_dots",
    )(xn, idx, table)


def _peer_mix(w, idx, table):
    t, hk = w.shape
    d = table.shape[1]
    per_worker, group, chunks = _sc_geometry(t, hk)
    steps = d // SC_LANES

    def body(w_hbm, idx_hbm, table_hbm, out_hbm, idx_v, w_v, o_v, ring, sems):
        wid = lax.axis_index("c") * SC_SUBCORES + lax.axis_index("s")
        lane = lax.iota(I32, SC_LANES)

        def compute(g, c, slot):
            wv = w_v[g, pl.ds(c * GATHER_ROWS, GATHER_ROWS)]
            ws = [jnp.full((SC_LANES,), jnp.sum(jnp.where(lane == r, wv, 0.0)), F32) for r in range(GATHER_ROWS)]

            @pl.loop(0, steps // LANE_STEPS)
            def _(cb):
                for u in range(LANE_STEPS):
                    col = (cb * LANE_STEPS + u) * SC_LANES
                    acc = ws[0] * ring[slot, 0, pl.ds(col, SC_LANES)]
                    for r in range(1, GATHER_ROWS):
                        acc = acc + ws[r] * ring[slot, r, pl.ds(col, SC_LANES)]
                    if c == 0:
                        o_v[g, pl.ds(col, SC_LANES)] = acc
                    else:
                        o_v[g, pl.ds(col, SC_LANES)] += acc

        @pl.loop(0, per_worker // group)
        def _(grp):
            tok0 = pl.multiple_of(wid * per_worker + grp * group, SUBLANES)
            pltpu.sync_copy(idx_hbm.at[pl.ds(tok0, group)], idx_v)
            pltpu.sync_copy(w_hbm.at[pl.ds(tok0, group)], w_v)
            _gather_ring(table_hbm, idx_v, ring, sems, group, chunks, compute)
            pltpu.sync_copy(o_v, out_hbm.at[pl.ds(tok0, group)])

    return pl.kernel(
        body,
        out_type=jax.ShapeDtypeStruct((t, d), F32),
        mesh=_sc_mesh(),
        scratch_types=[
            pltpu.VMEM((group, hk), I32),
            pltpu.VMEM((group, hk), F32),
            pltpu.VMEM((group, d), F32),
            pltpu.VMEM((GATHER_BUFFERS, GATHER_ROWS, d), F32),
            pltpu.SemaphoreType.DMA((GATHER_BUFFERS,)),
        ],
        compiler_params=pltpu.CompilerParams(needs_layout_passes=False),
        name="peer_mix",
    )(w, idx, table)


def _gate_body(d_ref, gate_ref, w_ref):
    x = d_ref[...]
    sqrt_half = np.sqrt(0.5).astype(np.float32)
    w_ref[...] = gate_ref[...] * (0.5 * x * (1.0 + lax.erf(x * sqrt_half)))


def _gate(dots, gate):
    t, hk = dots.shape
    tm = min(t, 2048)
    assert t % tm == 0
    spec = pl.BlockSpec((tm, hk), lambda i: (i, 0))
    return pl.pallas_call(
        _gate_body, grid=(t // tm,), in_specs=[spec, spec], out_specs=spec,
        out_shape=jax.ShapeDtypeStruct((t, hk), F32),
        compiler_params=pltpu.CompilerParams(dimension_semantics=("parallel",)),
        name="gate",
    )(dots, gate)


def _residual_body(h_ref, p_ref, g_ref, y_ref, *, final):
    x = h_ref[...] + p_ref[...]
    y_ref[...] = _rms(x, g_ref[...]) if final else x


def _residual(h2, p, g, final):
    t, d = h2.shape
    tm = min(t, 1024)
    assert t % tm == 0
    spec = pl.BlockSpec((tm, d), lambda i: (i, 0))
    return pl.pallas_call(
        functools.partial(_residual_body, final=final),
        grid=(t // tm,), in_specs=[spec, spec, _const_spec((1, d))], out_specs=spec,
        out_shape=jax.ShapeDtypeStruct((t, d), F32),
        compiler_params=pltpu.CompilerParams(dimension_semantics=("parallel",)),
        name="residual",
    )(h2, p, g.reshape(1, d))


def _trunk(x, hist_a, hist_b, norm_mix_g, w_in, conv_a_w, conv_b_w, conv_b_b, conv_norm_g, conv_norm_b,
           out_norm_a_g, out_norm_b_g, w_out, norm_ffn_g, w_q, sub_keys, u_experts, v_experts, final_norm_g):
    n, l, d = x.shape
    depth = w_in.shape[0]
    new_a, new_b = [], []
    for layer in range(depth):
        h, na, nb = _mixer(x, hist_a[layer], hist_b[layer], norm_mix_g[layer], w_in[layer], conv_a_w[layer],
                           conv_b_w[layer], conv_b_b[layer], conv_norm_g[layer], conv_norm_b[layer],
                           out_norm_a_g[layer], out_norm_b_g[layer], w_out[layer])
        h2 = h.reshape(n * l, d)
        xn, idx, gate = _route(h2, norm_ffn_g[layer], w_q[layer], sub_keys[layer])
        dots = _peer_dots(xn, idx, u_experts[layer])
        w = _gate(dots, gate)
        p = _peer_mix(w, idx, v_experts[layer])
        last = layer == depth - 1
        x = _residual(h2, p, final_norm_g if last else jnp.ones((d,), F32), final=last).reshape(n, l, d)
        new_a.append(na)
        new_b.append(nb)
    return x, jnp.stack(new_a), jnp.stack(new_b)


def kernel(x_prompt, x_sample, cache_conv_a, cache_conv_b, norm_mix_g, w_in, conv_a_w, conv_b_w, conv_b_b,
           conv_norm_g, conv_norm_b, out_norm_a_g, out_norm_b_g, w_out, norm_ffn_g, w_q, sub_keys,
           u_experts, v_experts, final_norm_g):
    weights = (norm_mix_g, w_in, conv_a_w, conv_b_w, conv_b_b, conv_norm_g, conv_norm_b, out_norm_a_g,
               out_norm_b_g, w_out, norm_ffn_g, w_q, sub_keys, u_experts, v_experts, final_norm_g)
    depth = w_in.shape[0]
    n = x_prompt.shape[0]
    zeros_a = jnp.zeros((depth, n) + cache_conv_a.shape[2:], x_prompt.dtype)
    zeros_b = jnp.zeros((depth, n) + cache_conv_b.shape[2:], x_prompt.dtype)
    y_prompt, conv_a_prompt, conv_b_prompt = _trunk(x_prompt, zeros_a, zeros_b, *weights)
    y_sample, conv_a_sample, conv_b_sample = _trunk(x_sample, cache_conv_a, cache_conv_b, *weights)
    return (y_prompt, y_sample, conv_a_prompt, conv_b_prompt, conv_a_sample, conv_b_sample)
```

```python
import functools

import numpy as np
import jax
import jax.numpy as jnp
from jax import lax
from jax.experimental import pallas as pl
from jax.experimental.pallas import tpu as pltpu
from jax.experimental.pallas import tpu_sc as plsc

F32 = jnp.float32
I32 = jnp.int32
HIGHEST = lax.Precision.HIGHEST
EPS = 1e-6

LANES = 128
SUBLANES = 8
VMEM_LIMIT_BYTES = 56 * 1024 * 1024
SC_CORES = 2
SC_SUBCORES = 16
SC_LANES = 16
SC_WORKERS = SC_CORES * SC_SUBCORES

PEER_TOPK = 16


def _rms(x, g):
    return x * lax.rsqrt(jnp.mean(x * x, axis=-1, keepdims=True) + EPS) * g


def _const_spec(shape):
    zeros = (0,) * len(shape)
    return pl.BlockSpec(shape, lambda *_: zeros, pipeline_mode=pl.Buffered(1))


CONV_ROW_BLOCK = 64


def _mixer_body(x_ref, ha_ref, hb_ref, gmix_ref, win_ref, caw_ref, cbw_ref, cbb_ref, cng_ref, cnb_ref,
                ga_ref, gb_ref, wout_ref, h_ref, na_ref, nb_ref, xa_s, xb_s, cb_s,
                *, tm, wa, wb, ka, kb, pad_a, pad_b):
    i = pl.program_id(1)

    @pl.when(i == 0)
    def _():
        xa_s[0:pad_a] = ha_ref[...]
        xb_s[0:pad_b] = hb_ref[...]

    x = x_ref[...]
    xn = _rms(x, gmix_ref[...])
    proj = jnp.dot(xn, win_ref[...], precision=HIGHEST, preferred_element_type=F32)
    h_a = proj[:, 0:wa]
    c_a = proj[:, wa:2 * wa]
    b_a = proj[:, 2 * wa:3 * wa]
    v_b = proj[:, 3 * wa:3 * wa + wb]
    g_b = proj[:, 3 * wa + wb:3 * wa + 2 * wb]

    xa_s[pad_a:pad_a + tm] = c_a * h_a
    conv_a = caw_ref[0:1, :] * xa_s[pad_a - (ka - 1):pad_a - (ka - 1) + tm]
    for k in range(1, ka):
        lo = pad_a - (ka - 1) + k
        conv_a = conv_a + caw_ref[k:k + 1, :] * xa_s[lo:lo + tm]
    y_a = _rms(b_a * conv_a, ga_ref[...])

    xb_s[pad_b:pad_b + tm] = v_b * jax.nn.sigmoid(g_b)
    rb = min(tm, CONV_ROW_BLOCK)
    for r0 in range(0, tm, rb):
        base = r0 + pad_b - (kb - 1)
        acc = cbw_ref[0:1, :] * xb_s[base:base + rb]
        for k in range(1, kb):
            acc = acc + cbw_ref[k:k + 1, :] * xb_s[base + k:base + k + rb]
        cb_s[r0:r0 + rb] = acc + cbb_ref[...]
    cb = cb_s[...]
    mu = jnp.mean(cb, axis=-1, keepdims=True)
    xc = cb - mu
    ln = xc * lax.rsqrt(jnp.mean(xc * xc, axis=-1, keepdims=True) + EPS) * cng_ref[...] + cnb_ref[...]
    z = ln * jax.nn.sigmoid(ln)
    y_b = _rms(z, gb_ref[...])

    y = (jnp.dot(y_a, wout_ref[0:wa, :], precision=HIGHEST, preferred_element_type=F32)
         + jnp.dot(y_b, wout_ref[wa:wa + wb, :], precision=HIGHEST, preferred_element_type=F32))
    h_ref[...] = x + y

    xa_s[0:pad_a] = xa_s[tm:tm + pad_a]
    xb_s[0:pad_b] = xb_s[tm:tm + pad_b]

    @pl.when(i == pl.num_programs(1) - 1)
    def _():
        na_ref[...] = xa_s[0:pad_a]
        nb_ref[...] = xb_s[0:pad_b]


def _mixer(x, seq0, n, hist_a, hist_b, gmix, w_in, caw, cbw, cbb, cng, cnb, ga, gb, w_out):
    _, l, d = x.shape
    ka, wa = caw.shape
    kb, wb = cbw.shape
    pad_a = SUBLANES
    pad_b = -(-(kb - 1) // SUBLANES) * SUBLANES
    tm = min(l, 512)
    assert l % tm == 0 and tm % min(tm, CONV_ROW_BLOCK) == 0
    assert tm >= pad_b and tm % SUBLANES == 0 and ka - 1 <= pad_a
    ha = jnp.pad(hist_a, ((0, 0), (pad_a - (ka - 1), 0), (0, 0)))
    hb = jnp.pad(hist_b, ((0, 0), (pad_b - (kb - 1), 0), (0, 0)))
    row = lambda v: v.reshape(1, -1)
    body = functools.partial(_mixer_body, tm=tm, wa=wa, wb=wb, ka=ka, kb=kb, pad_a=pad_a, pad_b=pad_b)
    h, na, nb = pl.pallas_call(
        body,
        grid=(n, l // tm),
        in_specs=[
            pl.BlockSpec((None, tm, d), lambda b, i: (seq0 + b, i, 0)),
            pl.BlockSpec((None, pad_a, wa), lambda b, i: (seq0 + b, 0, 0)),
            pl.BlockSpec((None, pad_b, wb), lambda b, i: (seq0 + b, 0, 0)),
            _const_spec((1, d)),
            _const_spec(w_in.shape),
            _const_spec(caw.shape),
            _const_spec(cbw.shape),
            _const_spec((1, wb)),
            _const_spec((1, wb)),
            _const_spec((1, wb)),
            _const_spec((1, wa)),
            _const_spec((1, wb)),
            _const_spec(w_out.shape),
        ],
        out_specs=[
            pl.BlockSpec((None, tm, d), lambda b, i: (b, i, 0)),
            pl.BlockSpec((None, pad_a, wa), lambda b, i: (b, 0, 0)),
            pl.BlockSpec((None, pad_b, wb), lambda b, i: (b, 0, 0)),
        ],
        out_shape=[
            jax.ShapeDtypeStruct((n, l, d), F32),
            jax.ShapeDtypeStruct((n, pad_a, wa), F32),
            jax.ShapeDtypeStruct((n, pad_b, wb), F32),
        ],
        scratch_shapes=[
            pltpu.VMEM((pad_a + tm, wa), F32),
            pltpu.VMEM((pad_b + tm, wb), F32),
            pltpu.VMEM((tm, wb), F32),
        ],
        compiler_params=pltpu.CompilerParams(
            dimension_semantics=("parallel", "arbitrary"), vmem_limit_bytes=VMEM_LIMIT_BYTES),
        name="mixer",
    )(x, ha, hb, row(gmix), w_in, caw, cbw, row(cbb), row(cng), row(cnb), row(ga), row(gb), w_out)
    return h, na[:, pad_a - (ka - 1):], nb[:, pad_b - (kb - 1):]


def _top_rows(s, k):
    n, lanes = s.shape
    rows = lax.broadcasted_iota(I32, s.shape, 0)
    slot = lax.broadcasted_iota(I32, (k, lanes), 0)
    vals = jnp.zeros((k, lanes), F32)
    ids = jnp.zeros((k, lanes), I32)
    for it in range(k):
        m = jnp.max(s, axis=0, keepdims=True)
        am = jnp.min(jnp.where(s == m, rows, n), axis=0, keepdims=True)
        vals = jnp.where(slot == it, m, vals)
        ids = jnp.where(slot == it, am, ids)
        s = jnp.where(rows == am, -jnp.inf, s)
    return vals, ids


def _combine(v1, i1, v2, i2, n_keys):
    k, lanes = v1.shape
    half = k // 2
    vals, pos, eid = [], [], []

    def add(a_lo, a_n, b_lo, b_n):
        vals.append(v1[a_lo:a_lo + a_n] + v2[b_lo:b_lo + b_n])
        eid.append(i1[a_lo:a_lo + a_n] * n_keys + i2[b_lo:b_lo + b_n])
        r = lax.broadcasted_iota(I32, (max(a_n, b_n), lanes), 0)
        pos.append((a_lo + r) * k + b_lo if a_n > 1 else a_lo * k + b_lo + r)

    add(0, 1, 0, k)
    for a in range(1, half):
        add(a, 1, 0, half)
    add(half, k - half, 0, 1)
    cand = jnp.concatenate(vals, axis=0)
    pos = jnp.concatenate(pos, axis=0)
    eid = jnp.concatenate(eid, axis=0)

    slot = lax.broadcasted_iota(I32, (k, lanes), 0)
    top_s = jnp.zeros((k, lanes), F32)
    top_e = jnp.zeros((k, lanes), I32)
    for it in range(k):
        m = jnp.max(cand, axis=0, keepdims=True)
        sel = jnp.min(jnp.where(cand == m, pos, k * k), axis=0, keepdims=True)
        hit = pos == sel
        e = jnp.max(jnp.where(hit, eid, -1), axis=0, keepdims=True)
        top_s = jnp.where(slot == it, m, top_s)
        top_e = jnp.where(slot == it, e, top_e)
        cand = jnp.where(hit, -jnp.inf, cand)
    return top_s, top_e


def _route_body(h_ref, g_ref, wq_ref, sk_ref, xn_ref, idx_ref, gate_ref, q_s, st_s, et_s, gt_s,
                *, tm, heads, n_keys, d_key):
    hn = _rms(h_ref[...], g_ref[...])
    xn_ref[...] = hn
    q_s[...] = jnp.dot(hn, wq_ref[...], precision=HIGHEST, preferred_element_type=F32)
    for hp in range(2 * heads):
        st_s[hp] = lax.dot_general(sk_ref[hp], q_s[:, hp * d_key:(hp + 1) * d_key],
                                   (((1,), (1,)), ((), ())), precision=HIGHEST, preferred_element_type=F32)
    n_chunk = tm // LANES

    def per_head_chunk(j, carry):
        hd = j // n_chunk
        col = pl.multiple_of((j % n_chunk) * LANES, LANES)
        v1, i1 = _top_rows(st_s[2 * hd, :, pl.ds(col, LANES)], PEER_TOPK)
        v2, i2 = _top_rows(st_s[2 * hd + 1, :, pl.ds(col, LANES)], PEER_TOPK)
        top_s, top_e = _combine(v1, i1, v2, i2, n_keys)
        p = jnp.exp(top_s - top_s[0:1])
        gate = p / jnp.sum(p, axis=0, keepdims=True)
        row = pl.multiple_of(hd * PEER_TOPK, PEER_TOPK)
        et_s[pl.ds(row, PEER_TOPK), pl.ds(col, LANES)] = top_e
        gt_s[pl.ds(row, PEER_TOPK), pl.ds(col, LANES)] = gate
        return carry

    lax.fori_loop(0, heads * n_chunk, per_head_chunk, 0)
    idx_ref[...] = et_s[...].T
    gate_ref[...] = gt_s[...].T


def _route(h2, g, w_q, sub_keys):
    t, d = h2.shape
    heads, _, n_keys, d_key = sub_keys.shape
    hk = heads * PEER_TOPK
    tm = min(t, 512)
    assert t % tm == 0 and tm % LANES == 0 and d_key == LANES and n_keys % SUBLANES == 0
    sk = sub_keys.reshape(2 * heads, n_keys, d_key)
    body = functools.partial(_route_body, tm=tm, heads=heads, n_keys=n_keys, d_key=d_key)
    return pl.pallas_call(
        body,
        grid=(t // tm,),
        in_specs=[
            pl.BlockSpec((tm, d), lambda i: (i, 0)),
            _const_spec((1, d)),
            _const_spec(w_q.shape),
            _const_spec(sk.shape),
        ],
        out_specs=[
            pl.BlockSpec((tm, d), lambda i: (i, 0)),
            pl.BlockSpec((tm, hk), lambda i: (i, 0)),
            pl.BlockSpec((tm, hk), lambda i: (i, 0)),
        ],
        out_shape=[
            jax.ShapeDtypeStruct((t, d), F32),
            jax.ShapeDtypeStruct((t, hk), I32),
            jax.ShapeDtypeStruct((t, hk), F32),
        ],
        scratch_shapes=[
            pltpu.VMEM((tm, 2 * heads * d_key), F32),
            pltpu.VMEM((2 * heads, n_keys, tm), F32),
            pltpu.VMEM((hk, tm), I32),
            pltpu.VMEM((hk, tm), F32),
        ],
        compiler_params=pltpu.CompilerParams(
            dimension_semantics=("parallel",), vmem_limit_bytes=VMEM_LIMIT_BYTES),
        name="route",
    )(h2, g.reshape(1, d), w_q, sk)


GATHER_ROWS = 16
GATHER_BUFFERS = 4
LANE_STEPS = 4


def _sc_mesh():
    return plsc.VectorSubcoreMesh(core_axis_name="c", subcore_axis_name="s")


def _sc_geometry(t, hk):
    per_worker = t // SC_WORKERS
    group = min(per_worker, 8)
    chunks = hk // GATHER_ROWS
    assert t % SC_WORKERS == 0 and per_worker % group == 0 and group % SUBLANES == 0
    assert hk % GATHER_ROWS == 0 and chunks % GATHER_BUFFERS == 0 and GATHER_ROWS == SC_LANES
    return per_worker, group, chunks


def _gather_ring(table_hbm, idx_v, ring, sems, group, chunks, compute):
    def copy(g, c, slot):
        rows = idx_v[g, pl.ds(c * GATHER_ROWS, GATHER_ROWS)]
        return pltpu.make_async_copy(table_hbm.at[rows], ring.at[slot], sems.at[slot])

    for c in range(GATHER_BUFFERS):
        copy(0, c, c).start()

    @pl.loop(0, group)
    def _(g):
        for c in range(chunks):
            slot = c % GATHER_BUFFERS
            copy(g, c, slot).wait()
            compute(g, c, slot)
            nxt = c + GATHER_BUFFERS
            if nxt < chunks:
                copy(g, nxt, slot).start()
            else:
                @pl.when(g + 1 < group)
                def _():
                    copy(g + 1, nxt - chunks, slot).start()


def _peer_dots(xn, idx, table):
    t, d = xn.shape
    hk = idx.shape[1]
    per_worker, group, chunks = _sc_geometry(t, hk)
    steps = d // SC_LANES

    def body(xn_hbm, idx_hbm, table_hbm, out_hbm, idx_v, x_v, o_v, ring, sems):
        wid = lax.axis_index("c") * SC_SUBCORES + lax.axis_index("s")
        lane = lax.iota(I32, SC_LANES)

        def compute(g, c, slot):
            def col_block(cb, accs):
                for u in range(LANE_STEPS):
                    col = (cb * LANE_STEPS + u) * SC_LANES
                    xc = x_v[g, pl.ds(col, SC_LANES)]
                    accs = tuple(a + ring[slot, r, pl.ds(col, SC_LANES)] * xc for r, a in enumerate(accs))
                return accs

            zero = jnp.zeros((SC_LANES,), F32)
            accs = lax.fori_loop(0, steps // LANE_STEPS, col_block, (zero,) * GATHER_ROWS)
            tot = zero
            for r in range(GATHER_ROWS):
                tot = jnp.where(lane == r, jnp.sum(accs[r]), tot)
            o_v[g, pl.ds(c * GATHER_ROWS, GATHER_ROWS)] = tot

        @pl.loop(0, per_worker // group)
        def _(grp):
            tok0 = pl.multiple_of(wid * per_worker + grp * group, SUBLANES)
            pltpu.sync_copy(idx_hbm.at[pl.ds(tok0, group)], idx_v)
            pltpu.sync_copy(xn_hbm.at[pl.ds(tok0, group)], x_v)
            _gather_ring(table_hbm, idx_v, ring, sems, group, chunks, compute)
            pltpu.sync_copy(o_v, out_hbm.at[pl.ds(tok0, group)])

    return pl.kernel(
        body,
        out_type=jax.ShapeDtypeStruct((t, hk), F32),
        mesh=_sc_mesh(),
        scratch_types=[
            pltpu.VMEM((group, hk), I32),
            pltpu.VMEM((group, d), F32),
            pltpu.VMEM((group, hk), F32),
            pltpu.VMEM((GATHER_BUFFERS, GATHER_ROWS, d), F32),
            pltpu.SemaphoreType.DMA((GATHER_BUFFERS,)),
        ],
        compiler_params=pltpu.CompilerParams(needs_layout_passes=False),
        name="peer_dots",
    )(xn, idx, table)


def _peer_mix(w, idx, table):
    t, hk = w.shape
    d = table.shape[1]
    per_worker, group, chunks = _sc_geometry(t, hk)
    steps = d // SC_LANES

    def body(w_hbm, idx_hbm, table_hbm, out_hbm, idx_v, w_v, o_v, ring, sems):
        wid = lax.axis_index("c") * SC_SUBCORES + lax.axis_index("s")
        lane = lax.iota(I32, SC_LANES)

        def compute(g, c, slot):
            wv = w_v[g, pl.ds(c * GATHER_ROWS, GATHER_ROWS)]
            ws = [jnp.full((SC_LANES,), jnp.sum(jnp.where(lane == r, wv, 0.0)), F32) for r in range(GATHER_ROWS)]

            @plsc.parallel_loop(0, steps // LANE_STEPS)
            def _(cb):
                sums = []
                for u in range(LANE_STEPS):
                    col = (cb * LANE_STEPS + u) * SC_LANES
                    terms = [ws[r] * ring[slot, r, pl.ds(col, SC_LANES)] for r in range(GATHER_ROWS)]
                    while len(terms) > 1:
                        terms = [a + b for a, b in zip(terms[0::2], terms[1::2])]
                    sums.append(terms[0])
                for u in range(LANE_STEPS):
                    col = (cb * LANE_STEPS + u) * SC_LANES
                    if c == 0:
                        o_v[g, pl.ds(col, SC_LANES)] = sums[u]
                    else:
                        o_v[g, pl.ds(col, SC_LANES)] += sums[u]

        @pl.loop(0, per_worker // group)
        def _(grp):
            tok0 = pl.multiple_of(wid * per_worker + grp * group, SUBLANES)
            pltpu.sync_copy(idx_hbm.at[pl.ds(tok0, group)], idx_v)
            pltpu.sync_copy(w_hbm.at[pl.ds(tok0, group)], w_v)
            _gather_ring(table_hbm, idx_v, ring, sems, group, chunks, compute)
            pltpu.sync_copy(o_v, out_hbm.at[pl.ds(tok0, group)])

    return pl.kernel(
        body,
        out_type=jax.ShapeDtypeStruct((t, d), F32),
        mesh=_sc_mesh(),
        scratch_types=[
            pltpu.VMEM((group, hk), I32),
            pltpu.VMEM((group, hk), F32),
            pltpu.VMEM((group, d), F32),
            pltpu.VMEM((GATHER_BUFFERS, GATHER_ROWS, d), F32),
            pltpu.SemaphoreType.DMA((GATHER_BUFFERS,)),
        ],
        compiler_params=pltpu.CompilerParams(needs_layout_passes=False),
        name="peer_mix",
    )(w, idx, table)


def _gate_body(d_ref, gate_ref, w_ref):
    x = d_ref[...]
    sqrt_half = np.sqrt(0.5).astype(np.float32)
    w_ref[...] = gate_ref[...] * (0.5 * x * (1.0 + lax.erf(x * sqrt_half)))


def _gate(dots, gate):
    t, hk = dots.shape
    tm = min(t, 2048)
    assert t % tm == 0
    spec = pl.BlockSpec((tm, hk), lambda i: (i, 0))
    return pl.pallas_call(
        _gate_body, grid=(t // tm,), in_specs=[spec, spec], out_specs=spec,
        out_shape=jax.ShapeDtypeStruct((t, hk), F32),
        compiler_params=pltpu.CompilerParams(dimension_semantics=("parallel",)),
        name="gate",
    )(dots, gate)


def _residual_body(h_ref, p_ref, g_ref, y_ref, *, final):
    x = h_ref[...] + p_ref[...]
    y_ref[...] = _rms(x, g_ref[...]) if final else x


def _residual(h2, p, g, final):
    t, d = h2.shape
    tm = min(t, 1024)
    assert t % tm == 0
    spec = pl.BlockSpec((tm, d), lambda i: (i, 0))
    return pl.pallas_call(
        functools.partial(_residual_body, final=final),
        grid=(t // tm,), in_specs=[spec, spec, _const_spec((1, d))], out_specs=spec,
        out_shape=jax.ShapeDtypeStruct((t, d), F32),
        compiler_params=pltpu.CompilerParams(dimension_semantics=("parallel",)),
        name="residual",
    )(h2, p, g.reshape(1, d))


SEQ_PARTS = 4


def _num_parts(n, l):
    per_part = (n // SEQ_PARTS) * l
    ok = n % SEQ_PARTS == 0 and per_part % (SC_WORKERS * SUBLANES) == 0
    return SEQ_PARTS if ok else 1


def _trunk(x, hist_a, hist_b, norm_mix_g, w_in, conv_a_w, conv_b_w, conv_b_b, conv_norm_g, conv_norm_b,
           out_norm_a_g, out_norm_b_g, w_out, norm_ffn_g, w_q, sub_keys, u_experts, v_experts, final_norm_g):
    n, l, d = x.shape
    depth = w_in.shape[0]
    parts = _num_parts(n, l)
    seqs = n // parts
    new_a, new_b = [], []
    for layer in range(depth):
        last = layer == depth - 1
        ys, nas, nbs = [], [], []
        for q in range(parts):
            h, na, nb = _mixer(x, q * seqs, seqs, hist_a[layer], hist_b[layer], norm_mix_g[layer], w_in[layer],
                               conv_a_w[layer], conv_b_w[layer], conv_b_b[layer], conv_norm_g[layer],
                               conv_norm_b[layer], out_norm_a_g[layer], out_norm_b_g[layer], w_out[layer])
            h2 = h.reshape(seqs * l, d)
            xn, idx, gate = _route(h2, norm_ffn_g[layer], w_q[layer], sub_keys[layer])
            dots = _peer_dots(xn, idx, u_experts[layer])
            w = _gate(dots, gate)
            p = _peer_mix(w, idx, v_experts[layer])
            y = _residual(h2, p, final_norm_g if last else jnp.ones((d,), F32), final=last)
            ys.append(y.reshape(seqs, l, d))
            nas.append(na)
            nbs.append(nb)
        x = jnp.concatenate(ys, axis=0)
        new_a.append(jnp.concatenate(nas, axis=0))
        new_b.append(jnp.concatenate(nbs, axis=0))
    return x, jnp.stack(new_a), jnp.stack(new_b)


def kernel(x_prompt, x_sample, cache_conv_a, cache_conv_b, norm_mix_g, w_in, conv_a_w, conv_b_w, conv_b_b,
           conv_norm_g, conv_norm_b, out_norm_a_g, out_norm_b_g, w_out, norm_ffn_g, w_q, sub_keys,
           u_experts, v_experts, final_norm_g):
    weights = (norm_mix_g, w_in, conv_a_w, conv_b_w, conv_b_b, conv_norm_g, conv_norm_b, out_norm_a_g,
               out_norm_b_g, w_out, norm_ffn_g, w_q, sub_keys, u_experts, v_experts, final_norm_g)
    depth = w_in.shape[0]
    n = x_prompt.shape[0]
    zeros_a = jnp.zeros((depth, n) + cache_conv_a.shape[2:], x_prompt.dtype)
    zeros_b = jnp.zeros((depth, n) + cache_conv_b.shape[2:], x_prompt.dtype)
    y_prompt, conv_a_prompt, conv_b_prompt = _trunk(x_prompt, zeros_a, zeros_b, *weights)
    y_sample, conv_a_sample, conv_b_sample = _trunk(x_sample, cache_conv_a, cache_conv_b, *weights)
    return (y_prompt, y_sample, conv_a_prompt, conv_b_prompt, conv_a_sample, conv_b_sample)
```

```python
import functools

import numpy as np
import jax
import jax.numpy as jnp
from jax import lax
from jax.experimental import pallas as pl
from jax.experimental.pallas import tpu as pltpu
from jax.experimental.pallas import tpu_sc as plsc

F32 = jnp.float32
I32 = jnp.int32
HIGHEST = lax.Precision.HIGHEST
EPS = 1e-6

LANES = 128
SUBLANES = 8
VMEM_LIMIT_BYTES = 56 * 1024 * 1024
SC_CORES = 2
SC_SUBCORES = 16
SC_LANES = 16
SC_WORKERS = SC_CORES * SC_SUBCORES

PEER_TOPK = 16


def _rms(x, g):
    return x * lax.rsqrt(jnp.mean(x * x, axis=-1, keepdims=True) + EPS) * g


def _const_spec(shape):
    zeros = (0,) * len(shape)
    return pl.BlockSpec(shape, lambda *_: zeros, pipeline_mode=pl.Buffered(1))


CONV_ROW_BLOCK = 64


def _mixer_body(x_ref, ha_ref, hb_ref, gmix_ref, win_ref, caw_ref, cbw_ref, cbb_ref, cng_ref, cnb_ref,
                ga_ref, gb_ref, wout_ref, h_ref, na_ref, nb_ref, xa_s, xb_s, cb_s,
                *, tm, wa, wb, ka, kb, pad_a, pad_b):
    i = pl.program_id(1)

    @pl.when(i == 0)
    def _():
        xa_s[0:pad_a] = ha_ref[...]
        xb_s[0:pad_b] = hb_ref[...]

    x = x_ref[...]
    xn = _rms(x, gmix_ref[...])
    proj = jnp.dot(xn, win_ref[...], precision=HIGHEST, preferred_element_type=F32)
    h_a = proj[:, 0:wa]
    c_a = proj[:, wa:2 * wa]
    b_a = proj[:, 2 * wa:3 * wa]
    v_b = proj[:, 3 * wa:3 * wa + wb]
    g_b = proj[:, 3 * wa + wb:3 * wa + 2 * wb]

    xa_s[pad_a:pad_a + tm] = c_a * h_a
    conv_a = caw_ref[0:1, :] * xa_s[pad_a - (ka - 1):pad_a - (ka - 1) + tm]
    for k in range(1, ka):
        lo = pad_a - (ka - 1) + k
        conv_a = conv_a + caw_ref[k:k + 1, :] * xa_s[lo:lo + tm]
    y_a = _rms(b_a * conv_a, ga_ref[...])

    xb_s[pad_b:pad_b + tm] = v_b * jax.nn.sigmoid(g_b)
    rb = min(tm, CONV_ROW_BLOCK)
    for r0 in range(0, tm, rb):
        base = r0 + pad_b - (kb - 1)
        acc = cbw_ref[0:1, :] * xb_s[base:base + rb]
        for k in range(1, kb):
            acc = acc + cbw_ref[k:k + 1, :] * xb_s[base + k:base + k + rb]
        cb_s[r0:r0 + rb] = acc + cbb_ref[...]
    cb = cb_s[...]
    mu = jnp.mean(cb, axis=-1, keepdims=True)
    xc = cb - mu
    ln = xc * lax.rsqrt(jnp.mean(xc * xc, axis=-1, keepdims=True) + EPS) * cng_ref[...] + cnb_ref[...]
    z = ln * jax.nn.sigmoid(ln)
    y_b = _rms(z, gb_ref[...])

    y = (jnp.dot(y_a, wout_ref[0:wa, :], precision=HIGHEST, preferred_element_type=F32)
         + jnp.dot(y_b, wout_ref[wa:wa + wb, :], precision=HIGHEST, preferred_element_type=F32))
    h_ref[...] = x + y

    xa_s[0:pad_a] = xa_s[tm:tm + pad_a]
    xb_s[0:pad_b] = xb_s[tm:tm + pad_b]

    @pl.when(i == pl.num_programs(1) - 1)
    def _():
        na_ref[...] = xa_s[0:pad_a]
        nb_ref[...] = xb_s[0:pad_b]


def _mixer(x, seq0, n, hist_a, hist_b, gmix, w_in, caw, cbw, cbb, cng, cnb, ga, gb, w_out):
    _, l, d = x.shape
    ka, wa = caw.shape
    kb, wb = cbw.shape
    pad_a = SUBLANES
    pad_b = -(-(kb - 1) // SUBLANES) * SUBLANES
    tm = min(l, 512)
    assert l % tm == 0 and tm % min(tm, CONV_ROW_BLOCK) == 0
    assert tm >= pad_b and tm % SUBLANES == 0 and ka - 1 <= pad_a
    ha = jnp.pad(hist_a, ((0, 0), (pad_a - (ka - 1), 0), (0, 0)))
    hb = jnp.pad(hist_b, ((0, 0), (pad_b - (kb - 1), 0), (0, 0)))
    row = lambda v: v.reshape(1, -1)
    body = functools.partial(_mixer_body, tm=tm, wa=wa, wb=wb, ka=ka, kb=kb, pad_a=pad_a, pad_b=pad_b)
    h, na, nb = pl.pallas_call(
        body,
        grid=(n, l // tm),
        in_specs=[
            pl.BlockSpec((None, tm, d), lambda b, i: (seq0 + b, i, 0)),
            pl.BlockSpec((None, pad_a, wa), lambda b, i: (seq0 + b, 0, 0)),
            pl.BlockSpec((None, pad_b, wb), lambda b, i: (seq0 + b, 0, 0)),
            _const_spec((1, d)),
            _const_spec(w_in.shape),
            _const_spec(caw.shape),
            _const_spec(cbw.shape),
            _const_spec((1, wb)),
            _const_spec((1, wb)),
            _const_spec((1, wb)),
            _const_spec((1, wa)),
            _const_spec((1, wb)),
            _const_spec(w_out.shape),
        ],
        out_specs=[
            pl.BlockSpec((None, tm, d), lambda b, i: (b, i, 0)),
            pl.BlockSpec((None, pad_a, wa), lambda b, i: (b, 0, 0)),
            pl.BlockSpec((None, pad_b, wb), lambda b, i: (b, 0, 0)),
        ],
        out_shape=[
            jax.ShapeDtypeStruct((n, l, d), F32),
            jax.ShapeDtypeStruct((n, pad_a, wa), F32),
            jax.ShapeDtypeStruct((n, pad_b, wb), F32),
        ],
        scratch_shapes=[
            pltpu.VMEM((pad_a + tm, wa), F32),
            pltpu.VMEM((pad_b + tm, wb), F32),
            pltpu.VMEM((tm, wb), F32),
        ],
        compiler_params=pltpu.CompilerParams(
            dimension_semantics=("parallel", "arbitrary"), vmem_limit_bytes=VMEM_LIMIT_BYTES),
        name="mixer",
    )(x, ha, hb, row(gmix), w_in, caw, cbw, row(cbb), row(cng), row(cnb), row(ga), row(gb), w_out)
    return h, na[:, pad_a - (ka - 1):], nb[:, pad_b - (kb - 1):]


def _top_rows(s, k):
    n, lanes = s.shape
    rows = lax.broadcasted_iota(I32, s.shape, 0)
    slot = lax.broadcasted_iota(I32, (k, lanes), 0)
    vals = jnp.zeros((k, lanes), F32)
    ids = jnp.zeros((k, lanes), I32)
    for it in range(k):
        m = jnp.max(s, axis=0, keepdims=True)
        am = jnp.min(jnp.where(s == m, rows, n), axis=0, keepdims=True)
        vals = jnp.where(slot == it, m, vals)
        ids = jnp.where(slot == it, am, ids)
        s = jnp.where(rows == am, -jnp.inf, s)
    return vals, ids


def _combine(v1, i1, v2, i2, n_keys):
    k, lanes = v1.shape
    half = k // 2
    vals, pos, eid = [], [], []

    def add(a_lo, a_n, b_lo, b_n):
        vals.append(v1[a_lo:a_lo + a_n] + v2[b_lo:b_lo + b_n])
        eid.append(i1[a_lo:a_lo + a_n] * n_keys + i2[b_lo:b_lo + b_n])
        r = lax.broadcasted_iota(I32, (max(a_n, b_n), lanes), 0)
        pos.append((a_lo + r) * k + b_lo if a_n > 1 else a_lo * k + b_lo + r)

    add(0, 1, 0, k)
    for a in range(1, half):
        add(a, 1, 0, half)
    add(half, k - half, 0, 1)
    cand = jnp.concatenate(vals, axis=0)
    pos = jnp.concatenate(pos, axis=0)
    eid = jnp.concatenate(eid, axis=0)

    slot = lax.broadcasted_iota(I32, (k, lanes), 0)
    top_s = jnp.zeros((k, lanes), F32)
    top_e = jnp.zeros((k, lanes), I32)
    for it in range(k):
        m = jnp.max(cand, axis=0, keepdims=True)
        sel = jnp.min(jnp.where(cand == m, pos, k * k), axis=0, keepdims=True)
        hit = pos == sel
        e = jnp.max(jnp.where(hit, eid, -1), axis=0, keepdims=True)
        top_s = jnp.where(slot == it, m, top_s)
        top_e = jnp.where(slot == it, e, top_e)
        cand = jnp.where(hit, -jnp.inf, cand)
    return top_s, top_e


def _route_body(h_ref, g_ref, wq_ref, sk_ref, xn_ref, idx_ref, gate_ref, q_s, st_s, et_s, gt_s,
                *, tm, heads, n_keys, d_key):
    hn = _rms(h_ref[...], g_ref[...])
    xn_ref[...] = hn
    q_s[...] = jnp.dot(hn, wq_ref[...], precision=HIGHEST, preferred_element_type=F32)
    for hp in range(2 * heads):
        st_s[hp] = lax.dot_general(sk_ref[hp], q_s[:, hp * d_key:(hp + 1) * d_key],
                                   (((1,), (1,)), ((), ())), precision=HIGHEST, preferred_element_type=F32)
    n_chunk = tm // LANES

    def per_head_chunk(j, carry):
        hd = j // n_chunk
        col = pl.multiple_of((j % n_chunk) * LANES, LANES)
        v1, i1 = _top_rows(st_s[2 * hd, :, pl.ds(col, LANES)], PEER_TOPK)
        v2, i2 = _top_rows(st_s[2 * hd + 1, :, pl.ds(col, LANES)], PEER_TOPK)
        top_s, top_e = _combine(v1, i1, v2, i2, n_keys)
        p = jnp.exp(top_s - top_s[0:1])
        gate = p / jnp.sum(p, axis=0, keepdims=True)
        row = pl.multiple_of(hd * PEER_TOPK, PEER_TOPK)
        et_s[pl.ds(row, PEER_TOPK), pl.ds(col, LANES)] = top_e
        gt_s[pl.ds(row, PEER_TOPK), pl.ds(col, LANES)] = gate
        return carry

    lax.fori_loop(0, heads * n_chunk, per_head_chunk, 0)
    idx_ref[...] = et_s[...].T
    gate_ref[...] = gt_s[...].T


def _route(h2, g, w_q, sub_keys):
    t, d = h2.shape
    heads, _, n_keys, d_key = sub_keys.shape
    hk = heads * PEER_TOPK
    tm = min(t, 512)
    assert t % tm == 0 and tm % LANES == 0 and d_key == LANES and n_keys % SUBLANES == 0
    sk = sub_keys.reshape(2 * heads, n_keys, d_key)
    body = functools.partial(_route_body, tm=tm, heads=heads, n_keys=n_keys, d_key=d_key)
    return pl.pallas_call(
        body,
        grid=(t // tm,),
        in_specs=[
            pl.BlockSpec((tm, d), lambda i: (i, 0)),
            _const_spec((1, d)),
            _const_spec(w_q.shape),
            _const_spec(sk.shape),
        ],
        out_specs=[
            pl.BlockSpec((tm, d), lambda i: (i, 0)),
            pl.BlockSpec((tm, hk), lambda i: (i, 0)),
            pl.BlockSpec((tm, hk), lambda i: (i, 0)),
        ],
        out_shape=[
            jax.ShapeDtypeStruct((t, d), F32),
            jax.ShapeDtypeStruct((t, hk), I32),
            jax.ShapeDtypeStruct((t, hk), F32),
        ],
        scratch_shapes=[
            pltpu.VMEM((tm, 2 * heads * d_key), F32),
            pltpu.VMEM((2 * heads, n_keys, tm), F32),
            pltpu.VMEM((hk, tm), I32),
            pltpu.VMEM((hk, tm), F32),
        ],
        compiler_params=pltpu.CompilerParams(
            dimension_semantics=("parallel",), vmem_limit_bytes=VMEM_LIMIT_BYTES),
        name="route",
    )(h2, g.reshape(1, d), w_q, sk)


GATHER_ROWS = 16
GATHER_BUFFERS = 4
LANE_STEPS = 4
GROUP_TOKENS = 32


def _sc_mesh():
    return plsc.VectorSubcoreMesh(core_axis_name="c", subcore_axis_name="s")


def _sc_geometry(t, hk):
    per_worker = t // SC_WORKERS
    group = min(per_worker, GROUP_TOKENS)
    chunks = hk // GATHER_ROWS
    assert t % SC_WORKERS == 0 and per_worker % group == 0 and group % SUBLANES == 0
    assert hk % GATHER_ROWS == 0 and chunks % GATHER_BUFFERS == 0 and GATHER_ROWS == SC_LANES
    return per_worker, group, chunks


def _gather_ring(table_hbm, idx_v, ring, sems, group, chunks, compute):
    def copy(g, c, slot):
        rows = idx_v[g, pl.ds(c * GATHER_ROWS, GATHER_ROWS)]
        return pltpu.make_async_copy(table_hbm.at[rows], ring.at[slot], sems.at[slot])

    for c in range(GATHER_BUFFERS):
        copy(0, c, c).start()

    @pl.loop(0, group)
    def _(g):
        for c in range(chunks):
            slot = c % GATHER_BUFFERS
            copy(g, c, slot).wait()
            compute(g, c, slot)
            nxt = c + GATHER_BUFFERS
            if nxt < chunks:
                copy(g, nxt, slot).start()
            else:
                @pl.when(g + 1 < group)
                def _():
                    copy(g + 1, nxt - chunks, slot).start()


def _peer_dots(xn, idx, table):
    t, d = xn.shape
    hk = idx.shape[1]
    per_worker, group, chunks = _sc_geometry(t, hk)
    steps = d // SC_LANES

    def body(xn_hbm, idx_hbm, table_hbm, out_hbm, idx_v, x_v, o_v, ring, sems):
        wid = lax.axis_index("c") * SC_SUBCORES + lax.axis_index("s")
        lane = lax.iota(I32, SC_LANES)

        def compute(g, c, slot):
            def col_block(cb, accs):
                for u in range(LANE_STEPS):
                    col = (cb * LANE_STEPS + u) * SC_LANES
                    xc = x_v[g, pl.ds(col, SC_LANES)]
                    accs = tuple(a + ring[slot, r, pl.ds(col, SC_LANES)] * xc for r, a in enumerate(accs))
                return accs

            zero = jnp.zeros((SC_LANES,), F32)
            accs = lax.fori_loop(0, steps // LANE_STEPS, col_block, (zero,) * GATHER_ROWS)
            tot = zero
            for r in range(GATHER_ROWS):
                tot = jnp.where(lane == r, jnp.sum(accs[r]), tot)
            o_v[g, pl.ds(c * GATHER_ROWS, GATHER_ROWS)] = tot

        @pl.loop(0, per_worker // group)
        def _(grp):
            tok0 = pl.multiple_of(wid * per_worker + grp * group, SUBLANES)
            pltpu.sync_copy(idx_hbm.at[pl.ds(tok0, group)], idx_v)
            pltpu.sync_copy(xn_hbm.at[pl.ds(tok0, group)], x_v)
            _gather_ring(table_hbm, idx_v, ring, sems, group, chunks, compute)
            pltpu.sync_copy(o_v, out_hbm.at[pl.ds(tok0, group)])

    return pl.kernel(
        body,
        out_type=jax.ShapeDtypeStruct((t, hk), F32),
        mesh=_sc_mesh(),
        scratch_types=[
            pltpu.VMEM((group, hk), I32),
            pltpu.VMEM((group, d), F32),
            pltpu.VMEM((group, hk), F32),
            pltpu.VMEM((GATHER_BUFFERS, GATHER_ROWS, d), F32),
            pltpu.SemaphoreType.DMA((GATHER_BUFFERS,)),
        ],
        compiler_params=pltpu.CompilerParams(needs_layout_passes=False),
        name="peer_dots",
    )(xn, idx, table)


def _peer_mix(w, idx, table):
    t, hk = w.shape
    d = table.shape[1]
    per_worker, group, chunks = _sc_geometry(t, hk)
    steps = d // SC_LANES

    def body(w_hbm, idx_hbm, table_hbm, out_hbm, idx_v, w_v, o_v, ring, sems):
        wid = lax.axis_index("c") * SC_SUBCORES + lax.axis_index("s")
        lane = lax.iota(I32, SC_LANES)

        def compute(g, c, slot):
            wv = w_v[g, pl.ds(c * GATHER_ROWS, GATHER_ROWS)]
            ws = [jnp.full((SC_LANES,), jnp.sum(jnp.where(lane == r, wv, 0.0)), F32) for r in range(GATHER_ROWS)]

            @plsc.parallel_loop(0, steps // LANE_STEPS)
            def _(cb):
                sums = []
                for u in range(LANE_STEPS):
                    col = (cb * LANE_STEPS + u) * SC_LANES
                    terms = [ws[r] * ring[slot, r, pl.ds(col, SC_LANES)] for r in range(GATHER_ROWS)]
                    while len(terms) > 1:
                        terms = [a + b for a, b in zip(terms[0::2], terms[1::2])]
                    sums.append(terms[0])
                for u in range(LANE_STEPS):
                    col = (cb * LANE_STEPS + u) * SC_LANES
                    if c == 0:
                        o_v[g, pl.ds(col, SC_LANES)] = sums[u]
                    else:
                        o_v[g, pl.ds(col, SC_LANES)] += sums[u]

        @pl.loop(0, per_worker // group)
        def _(grp):
            tok0 = pl.multiple_of(wid * per_worker + grp * group, SUBLANES)
            pltpu.sync_copy(idx_hbm.at[pl.ds(tok0, group)], idx_v)
            pltpu.sync_copy(w_hbm.at[pl.ds(tok0, group)], w_v)
            _gather_ring(table_hbm, idx_v, ring, sems, group, chunks, compute)
            pltpu.sync_copy(o_v, out_hbm.at[pl.ds(tok0, group)])

    return pl.kernel(
        body,
        out_type=jax.ShapeDtypeStruct((t, d), F32),
        mesh=_sc_mesh(),
        scratch_types=[
            pltpu.VMEM((group, hk), I32),
            pltpu.VMEM((group, hk), F32),
            pltpu.VMEM((group, d), F32),
            pltpu.VMEM((GATHER_BUFFERS, GATHER_ROWS, d), F32),
            pltpu.SemaphoreType.DMA((GATHER_BUFFERS,)),
        ],
        compiler_params=pltpu.CompilerParams(needs_layout_passes=False),
        name="peer_mix",
    )(w, idx, table)


def _gate_body(d_ref, gate_ref, w_ref):
    x = d_ref[...]
    sqrt_half = np.sqrt(0.5).astype(np.float32)
    w_ref[...] = gate_ref[...] * (0.5 * x * (1.0 + lax.erf(x * sqrt_half)))


def _gate(dots, gate):
    t, hk = dots.shape
    tm = min(t, 2048)
    assert t % tm == 0
    spec = pl.BlockSpec((tm, hk), lambda i: (i, 0))
    return pl.pallas_call(
        _gate_body, grid=(t // tm,), in_specs=[spec, spec], out_specs=spec,
        out_shape=jax.ShapeDtypeStruct((t, hk), F32),
        compiler_params=pltpu.CompilerParams(dimension_semantics=("parallel",)),
        name="gate",
    )(dots, gate)


def _residual_body(h_ref, p_ref, g_ref, y_ref, *, final):
    x = h_ref[...] + p_ref[...]
    y_ref[...] = _rms(x, g_ref[...]) if final else x


def _residual(h2, p, g, final):
    t, d = h2.shape
    tm = min(t, 1024)
    assert t % tm == 0
    spec = pl.BlockSpec((tm, d), lambda i: (i, 0))
    return pl.pallas_call(
        functools.partial(_residual_body, final=final),
        grid=(t // tm,), in_specs=[spec, spec, _const_spec((1, d))], out_specs=spec,
        out_shape=jax.ShapeDtypeStruct((t, d), F32),
        compiler_params=pltpu.CompilerParams(dimension_semantics=("parallel",)),
        name="residual",
    )(h2, p, g.reshape(1, d))


SEQ_PARTS = 8


def _num_parts(n, l):
    per_part = (n // SEQ_PARTS) * l
    ok = n % SEQ_PARTS == 0 and per_part % (SC_WORKERS * SUBLANES) == 0
    return SEQ_PARTS if ok else 1


def _trunk(x, hist_a, hist_b, norm_mix_g, w_in, conv_a_w, conv_b_w, conv_b_b, conv_norm_g, conv_norm_b,
           out_norm_a_g, out_norm_b_g, w_out, norm_ffn_g, w_q, sub_keys, u_experts, v_experts, final_norm_g):
    n, l, d = x.shape
    depth = w_in.shape[0]
    parts = _num_parts(n, l)
    seqs = n // parts
    new_a, new_b = [], []
    for layer in range(depth):
        last = layer == depth - 1
        ys, nas, nbs = [], [], []
        for q in range(parts):
            h, na, nb = _mixer(x, q * seqs, seqs, hist_a[layer], hist_b[layer], norm_mix_g[layer], w_in[layer],
                               conv_a_w[layer], conv_b_w[layer], conv_b_b[layer], conv_norm_g[layer],
                               conv_norm_b[layer], out_norm_a_g[layer], out_norm_b_g[layer], w_out[layer])
            h2 = h.reshape(seqs * l, d)
            xn, idx, gate = _route(h2, norm_ffn_g[layer], w_q[layer], sub_keys[layer])
            dots = _peer_dots(xn, idx, u_experts[layer])
            w = _gate(dots, gate)
            p = _peer_mix(w, idx, v_experts[layer])
            y = _residual(h2, p, final_norm_g if last else jnp.ones((d,), F32), final=last)
            ys.append(y.reshape(seqs, l, d))
            nas.append(na)
            nbs.append(nb)
        x = jnp.concatenate(ys, axis=0)
        new_a.append(jnp.concatenate(nas, axis=0))
        new_b.append(jnp.concatenate(nbs, axis=0))
    return x, jnp.stack(new_a), jnp.stack(new_b)


def kernel(x_prompt, x_sample, cache_conv_a, cache_conv_b, norm_mix_g, w_in, conv_a_w, conv_b_w, conv_b_b,
           conv_norm_g, conv_norm_b, out_norm_a_g, out_norm_b_g, w_out, norm_ffn_g, w_q, sub_keys,
           u_experts, v_experts, final_norm_g):
    weights = (norm_mix_g, w_in, conv_a_w, conv_b_w, conv_b_b, conv_norm_g, conv_norm_b, out_norm_a_g,
               out_norm_b_g, w_out, norm_ffn_g, w_q, sub_keys, u_experts, v_experts, final_norm_g)
    depth = w_in.shape[0]
    n = x_prompt.shape[0]
    zeros_a = jnp.zeros((depth, n) + cache_conv_a.shape[2:], x_prompt.dtype)
    zeros_b = jnp.zeros((depth, n) + cache_conv_b.shape[2:], x_prompt.dtype)
    y_prompt, conv_a_prompt, conv_b_prompt = _trunk(x_prompt, zeros_a, zeros_b, *weights)
    y_sample, conv_a_sample, conv_b_sample = _trunk(x_sample, cache_conv_a, cache_conv_b, *weights)
    return (y_prompt, y_sample, conv_a_prompt, conv_b_prompt, conv_a_sample, conv_b_sample)
```

```python
import functools

import numpy as np
import jax
import jax.numpy as jnp
from jax import lax
from jax.experimental import pallas as pl
from jax.experimental.pallas import tpu as pltpu
from jax.experimental.pallas import tpu_sc as plsc

F32 = jnp.float32
I32 = jnp.int32
HIGHEST = lax.Precision.HIGHEST
EPS = 1e-6

LANES = 128
SUBLANES = 8
VMEM_LIMIT_BYTES = 56 * 1024 * 1024
SC_CORES = 2
SC_SUBCORES = 16
SC_LANES = 16
SC_WORKERS = SC_CORES * SC_SUBCORES

PEER_TOPK = 16


def _rms(x, g):
    return x * lax.rsqrt(jnp.mean(x * x, axis=-1, keepdims=True) + EPS) * g


def _const_spec(shape):
    zeros = (0,) * len(shape)
    return pl.BlockSpec(shape, lambda *_: zeros, pipeline_mode=pl.Buffered(1))


CONV_ROW_BLOCK = 64


def _mixer_body(x_ref, ha_ref, hb_ref, gmix_ref, win_ref, caw_ref, cbw_ref, cbb_ref, cng_ref, cnb_ref,
                ga_ref, gb_ref, wout_ref, h_ref, na_ref, nb_ref, xa_s, xb_s, cb_s,
                *, tm, wa, wb, ka, kb, pad_a, pad_b):
    i = pl.program_id(1)

    @pl.when(i == 0)
    def _():
        xa_s[0:pad_a] = ha_ref[...]
        xb_s[0:pad_b] = hb_ref[...]

    x = x_ref[...]
    xn = _rms(x, gmix_ref[...])
    proj = jnp.dot(xn, win_ref[...], precision=HIGHEST, preferred_element_type=F32)
    h_a = proj[:, 0:wa]
    c_a = proj[:, wa:2 * wa]
    b_a = proj[:, 2 * wa:3 * wa]
    v_b = proj[:, 3 * wa:3 * wa + wb]
    g_b = proj[:, 3 * wa + wb:3 * wa + 2 * wb]

    xa_s[pad_a:pad_a + tm] = c_a * h_a
    conv_a = caw_ref[0:1, :] * xa_s[pad_a - (ka - 1):pad_a - (ka - 1) + tm]
    for k in range(1, ka):
        lo = pad_a - (ka - 1) + k
        conv_a = conv_a + caw_ref[k:k + 1, :] * xa_s[lo:lo + tm]
    y_a = _rms(b_a * conv_a, ga_ref[...])

    xb_s[pad_b:pad_b + tm] = v_b * jax.nn.sigmoid(g_b)
    rb = min(tm, CONV_ROW_BLOCK)
    for r0 in range(0, tm, rb):
        base = r0 + pad_b - (kb - 1)
        acc = cbw_ref[0:1, :] * xb_s[base:base + rb]
        for k in range(1, kb):
            acc = acc + cbw_ref[k:k + 1, :] * xb_s[base + k:base + k + rb]
        cb_s[r0:r0 + rb] = acc + cbb_ref[...]
    cb = cb_s[...]
    mu = jnp.mean(cb, axis=-1, keepdims=True)
    xc = cb - mu
    ln = xc * lax.rsqrt(jnp.mean(xc * xc, axis=-1, keepdims=True) + EPS) * cng_ref[...] + cnb_ref[...]
    z = ln * jax.nn.sigmoid(ln)
    y_b = _rms(z, gb_ref[...])

    y = (jnp.dot(y_a, wout_ref[0:wa, :], precision=HIGHEST, preferred_element_type=F32)
         + jnp.dot(y_b, wout_ref[wa:wa + wb, :], precision=HIGHEST, preferred_element_type=F32))
    h_ref[...] = x + y

    xa_s[0:pad_a] = xa_s[tm:tm + pad_a]
    xb_s[0:pad_b] = xb_s[tm:tm + pad_b]

    @pl.when(i == pl.num_programs(1) - 1)
    def _():
        na_ref[...] = xa_s[0:pad_a]
        nb_ref[...] = xb_s[0:pad_b]


def _mixer(x, seq0, n, hist_a, hist_b, gmix, w_in, caw, cbw, cbb, cng, cnb, ga, gb, w_out):
    _, l, d = x.shape
    ka, wa = caw.shape
    kb, wb = cbw.shape
    pad_a = SUBLANES
    pad_b = -(-(kb - 1) // SUBLANES) * SUBLANES
    tm = min(l, 512)
    assert l % tm == 0 and tm % min(tm, CONV_ROW_BLOCK) == 0
    assert tm >= pad_b and tm % SUBLANES == 0 and ka - 1 <= pad_a
    ha = jnp.pad(hist_a, ((0, 0), (pad_a - (ka - 1), 0), (0, 0)))
    hb = jnp.pad(hist_b, ((0, 0), (pad_b - (kb - 1), 0), (0, 0)))
    row = lambda v: v.reshape(1, -1)
    body = functools.partial(_mixer_body, tm=tm, wa=wa, wb=wb, ka=ka, kb=kb, pad_a=pad_a, pad_b=pad_b)
    h, na, nb = pl.pallas_call(
        body,
        grid=(n, l // tm),
        in_specs=[
            pl.BlockSpec((None, tm, d), lambda b, i: (seq0 + b, i, 0)),
            pl.BlockSpec((None, pad_a, wa), lambda b, i: (seq0 + b, 0, 0)),
            pl.BlockSpec((None, pad_b, wb), lambda b, i: (seq0 + b, 0, 0)),
            _const_spec((1, d)),
            _const_spec(w_in.shape),
            _const_spec(caw.shape),
            _const_spec(cbw.shape),
            _const_spec((1, wb)),
            _const_spec((1, wb)),
            _const_spec((1, wb)),
            _const_spec((1, wa)),
            _const_spec((1, wb)),
            _const_spec(w_out.shape),
        ],
        out_specs=[
            pl.BlockSpec((None, tm, d), lambda b, i: (b, i, 0)),
            pl.BlockSpec((None, pad_a, wa), lambda b, i: (b, 0, 0)),
            pl.BlockSpec((None, pad_b, wb), lambda b, i: (b, 0, 0)),
        ],
        out_shape=[
            jax.ShapeDtypeStruct((n, l, d), F32),
            jax.ShapeDtypeStruct((n, pad_a, wa), F32),
            jax.ShapeDtypeStruct((n, pad_b, wb), F32),
        ],
        scratch_shapes=[
            pltpu.VMEM((pad_a + tm, wa), F32),
            pltpu.VMEM((pad_b + tm, wb), F32),
            pltpu.VMEM((tm, wb), F32),
        ],
        compiler_params=pltpu.CompilerParams(
            dimension_semantics=("parallel", "arbitrary"), vmem_limit_bytes=VMEM_LIMIT_BYTES),
        name="mixer",
    )(x, ha, hb, row(gmix), w_in, caw, cbw, row(cbb), row(cng), row(cnb), row(ga), row(gb), w_out)
    return h, na[:, pad_a - (ka - 1):], nb[:, pad_b - (kb - 1):]


def _top_rows(s, k):
    n, lanes = s.shape
    rows = lax.broadcasted_iota(I32, s.shape, 0)
    slot = lax.broadcasted_iota(I32, (k, lanes), 0)
    vals = jnp.zeros((k, lanes), F32)
    ids = jnp.zeros((k, lanes), I32)
    for it in range(k):
        m = jnp.max(s, axis=0, keepdims=True)
        am = jnp.min(jnp.where(s == m, rows, n), axis=0, keepdims=True)
        vals = jnp.where(slot == it, m, vals)
        ids = jnp.where(slot == it, am, ids)
        s = jnp.where(rows == am, -jnp.inf, s)
    return vals, ids


def _combine(v1, i1, v2, i2, n_keys):
    k, lanes = v1.shape
    half = k // 2
    vals, pos, eid = [], [], []

    def add(a_lo, a_n, b_lo, b_n):
        vals.append(v1[a_lo:a_lo + a_n] + v2[b_lo:b_lo + b_n])
        eid.append(i1[a_lo:a_lo + a_n] * n_keys + i2[b_lo:b_lo + b_n])
        r = lax.broadcasted_iota(I32, (max(a_n, b_n), lanes), 0)
        pos.append((a_lo + r) * k + b_lo if a_n > 1 else a_lo * k + b_lo + r)

    add(0, 1, 0, k)
    for a in range(1, half):
        add(a, 1, 0, half)
    add(half, k - half, 0, 1)
    cand = jnp.concatenate(vals, axis=0)
    pos = jnp.concatenate(pos, axis=0)
    eid = jnp.concatenate(eid, axis=0)

    slot = lax.broadcasted_iota(I32, (k, lanes), 0)
    top_s = jnp.zeros((k, lanes), F32)
    top_e = jnp.zeros((k, lanes), I32)
    for it in range(k):
        m = jnp.max(cand, axis=0, keepdims=True)
        sel = jnp.min(jnp.where(cand == m, pos, k * k), axis=0, keepdims=True)
        hit = pos == sel
        e = jnp.max(jnp.where(hit, eid, -1), axis=0, keepdims=True)
        top_s = jnp.where(slot == it, m, top_s)
        top_e = jnp.where(slot == it, e, top_e)
        cand = jnp.where(hit, -jnp.inf, cand)
    return top_s, top_e


def _route_body(h_ref, g_ref, wq_ref, sk_ref, xn_ref, idx_ref, gate_ref, q_s, st_s, et_s, gt_s,
                *, tm, heads, n_keys, d_key):
    hn = _rms(h_ref[...], g_ref[...])
    xn_ref[...] = hn
    q_s[...] = jnp.dot(hn, wq_ref[...], precision=HIGHEST, preferred_element_type=F32)
    for hp in range(2 * heads):
        st_s[hp] = lax.dot_general(sk_ref[hp], q_s[:, hp * d_key:(hp + 1) * d_key],
                                   (((1,), (1,)), ((), ())), precision=HIGHEST, preferred_element_type=F32)
    n_chunk = tm // LANES

    def per_head_chunk(j, carry):
        hd = j // n_chunk
        col = pl.multiple_of((j % n_chunk) * LANES, LANES)
        v1, i1 = _top_rows(st_s[2 * hd, :, pl.ds(col, LANES)], PEER_TOPK)
        v2, i2 = _top_rows(st_s[2 * hd + 1, :, pl.ds(col, LANES)], PEER_TOPK)
        top_s, top_e = _combine(v1, i1, v2, i2, n_keys)
        p = jnp.exp(top_s - top_s[0:1])
        gate = p / jnp.sum(p, axis=0, keepdims=True)
        row = pl.multiple_of(hd * PEER_TOPK, PEER_TOPK)
        et_s[pl.ds(row, PEER_TOPK), pl.ds(col, LANES)] = top_e
        gt_s[pl.ds(row, PEER_TOPK), pl.ds(col, LANES)] = gate
        return carry

    lax.fori_loop(0, heads * n_chunk, per_head_chunk, 0)
    idx_ref[...] = et_s[...].T
    gate_ref[...] = gt_s[...].T


def _route(h2, g, w_q, sub_keys):
    t, d = h2.shape
    heads, _, n_keys, d_key = sub_keys.shape
    hk = heads * PEER_TOPK
    tm = min(t, 512)
    assert t % tm == 0 and tm % LANES == 0 and d_key == LANES and n_keys % SUBLANES == 0
    sk = sub_keys.reshape(2 * heads, n_keys, d_key)
    body = functools.partial(_route_body, tm=tm, heads=heads, n_keys=n_keys, d_key=d_key)
    return pl.pallas_call(
        body,
        grid=(t // tm,),
        in_specs=[
            pl.BlockSpec((tm, d), lambda i: (i, 0)),
            _const_spec((1, d)),
            _const_spec(w_q.shape),
            _const_spec(sk.shape),
        ],
        out_specs=[
            pl.BlockSpec((tm, d), lambda i: (i, 0)),
            pl.BlockSpec((tm, hk), lambda i: (i, 0)),
            pl.BlockSpec((tm, hk), lambda i: (i, 0)),
        ],
        out_shape=[
            jax.ShapeDtypeStruct((t, d), F32),
            jax.ShapeDtypeStruct((t, hk), I32),
            jax.ShapeDtypeStruct((t, hk), F32),
        ],
        scratch_shapes=[
            pltpu.VMEM((tm, 2 * heads * d_key), F32),
            pltpu.VMEM((2 * heads, n_keys, tm), F32),
            pltpu.VMEM((hk, tm), I32),
            pltpu.VMEM((hk, tm), F32),
        ],
        compiler_params=pltpu.CompilerParams(
            dimension_semantics=("parallel",), vmem_limit_bytes=VMEM_LIMIT_BYTES),
        name="route",
    )(h2, g.reshape(1, d), w_q, sk)


GATHER_ROWS = 16
GATHER_BUFFERS = 4
LANE_STEPS = 4
GROUP_TOKENS = 16


def _sc_mesh():
    return plsc.VectorSubcoreMesh(core_axis_name="c", subcore_axis_name="s")


def _sc_geometry(t, hk):
    per_worker = t // SC_WORKERS
    group = min(per_worker, GROUP_TOKENS)
    chunks = hk // GATHER_ROWS
    assert t % SC_WORKERS == 0 and per_worker % group == 0 and group % SUBLANES == 0
    assert hk % GATHER_ROWS == 0 and chunks % GATHER_BUFFERS == 0 and GATHER_ROWS == SC_LANES
    return per_worker, group, chunks


def _token_stream(first_token, per_worker, group, chunks, idx_hbm, side_hbm, table_hbm, out_hbm,
                  idx_v, side_v, o_v, ring, sems, stage_sems, compute):
    n_groups = per_worker // group

    def stage(grp, par):
        tok0 = pl.multiple_of(first_token + grp * group, SUBLANES)
        return (pltpu.make_async_copy(idx_hbm.at[pl.ds(tok0, group)], idx_v.at[par], stage_sems.at[par]),
                pltpu.make_async_copy(side_hbm.at[pl.ds(tok0, group)], side_v.at[par], stage_sems.at[par]))

    def gather(par, g, c, slot):
        rows = idx_v[par, g, pl.ds(c * GATHER_ROWS, GATHER_ROWS)]
        return pltpu.make_async_copy(table_hbm.at[rows], ring.at[slot], sems.at[slot])

    for cp in stage(0, 0):
        cp.start()
    for cp in stage(0, 0):
        cp.wait()
    for c in range(GATHER_BUFFERS):
        gather(0, 0, c, c).start()

    @pl.loop(0, per_worker)
    def _(tk):
        grp = tk // group
        g = tk % group
        par = grp % 2
        more_groups = grp + 1 < n_groups

        @pl.when((g == 0) & more_groups)
        def _():
            for cp in stage(grp + 1, 1 - par):
                cp.start()

        @pl.when((g == group - 1) & more_groups)
        def _():
            for cp in stage(grp + 1, 1 - par):
                cp.wait()

        for c in range(chunks):
            slot = c % GATHER_BUFFERS
            gather(par, g, c, slot).wait()
            compute(par, g, c, slot)
            nxt = c + GATHER_BUFFERS
            if nxt < chunks:
                gather(par, g, nxt, slot).start()
            else:
                @pl.when(tk + 1 < per_worker)
                def _():
                    gather(((tk + 1) // group) % 2, (tk + 1) % group, nxt - chunks, slot).start()

        @pl.when(g == group - 1)
        def _():
            tok0 = pl.multiple_of(first_token + grp * group, SUBLANES)
            pltpu.sync_copy(o_v, out_hbm.at[pl.ds(tok0, group)])


def _peer_dots(xn, idx, table):
    t, d = xn.shape
    hk = idx.shape[1]
    per_worker, group, chunks = _sc_geometry(t, hk)
    steps = d // SC_LANES

    def body(xn_hbm, idx_hbm, table_hbm, out_hbm, idx_v, x_v, o_v, ring, sems, stage_sems):
        wid = lax.axis_index("c") * SC_SUBCORES + lax.axis_index("s")
        lane = lax.iota(I32, SC_LANES)

        def compute(par, g, c, slot):
            def col_block(cb, accs):
                for u in range(LANE_STEPS):
                    col = (cb * LANE_STEPS + u) * SC_LANES
                    xc = x_v[par, g, pl.ds(col, SC_LANES)]
                    accs = tuple(a + ring[slot, r, pl.ds(col, SC_LANES)] * xc for r, a in enumerate(accs))
                return accs

            zero = jnp.zeros((SC_LANES,), F32)
            accs = lax.fori_loop(0, steps // LANE_STEPS, col_block, (zero,) * GATHER_ROWS)
            tot = zero
            for r in range(GATHER_ROWS):
                tot = jnp.where(lane == r, jnp.sum(accs[r]), tot)
            o_v[g, pl.ds(c * GATHER_ROWS, GATHER_ROWS)] = tot

        _token_stream(wid * per_worker, per_worker, group, chunks, idx_hbm, xn_hbm, table_hbm, out_hbm,
                      idx_v, x_v, o_v, ring, sems, stage_sems, compute)

    return pl.kernel(
        body,
        out_type=jax.ShapeDtypeStruct((t, hk), F32),
        mesh=_sc_mesh(),
        scratch_types=[
            pltpu.VMEM((2, group, hk), I32),
            pltpu.VMEM((2, group, d), F32),
            pltpu.VMEM((group, hk), F32),
            pltpu.VMEM((GATHER_BUFFERS, GATHER_ROWS, d), F32),
            pltpu.SemaphoreType.DMA((GATHER_BUFFERS,)),
            pltpu.SemaphoreType.DMA((2,)),
        ],
        compiler_params=pltpu.CompilerParams(needs_layout_passes=False),
        name="peer_dots",
    )(xn, idx, table)


def _peer_mix(w, idx, table):
    t, hk = w.shape
    d = table.shape[1]
    per_worker, group, chunks = _sc_geometry(t, hk)
    steps = d // SC_LANES

    def body(w_hbm, idx_hbm, table_hbm, out_hbm, idx_v, w_v, o_v, ring, sems, stage_sems):
        wid = lax.axis_index("c") * SC_SUBCORES + lax.axis_index("s")
        lane = lax.iota(I32, SC_LANES)

        def compute(par, g, c, slot):
            wv = w_v[par, g, pl.ds(c * GATHER_ROWS, GATHER_ROWS)]
            ws = [jnp.full((SC_LANES,), jnp.sum(jnp.where(lane == r, wv, 0.0)), F32) for r in range(GATHER_ROWS)]

            @plsc.parallel_loop(0, steps // LANE_STEPS)
            def _(cb):
                sums = []
                for u in range(LANE_STEPS):
                    col = (cb * LANE_STEPS + u) * SC_LANES
                    terms = [ws[r] * ring[slot, r, pl.ds(col, SC_LANES)] for r in range(GATHER_ROWS)]
                    while len(terms) > 1:
                        terms = [a + b for a, b in zip(terms[0::2], terms[1::2])]
                    sums.append(terms[0])
                for u in range(LANE_STEPS):
                    col = (cb * LANE_STEPS + u) * SC_LANES
                    if c == 0:
                        o_v[g, pl.ds(col, SC_LANES)] = sums[u]
                    else:
                        o_v[g, pl.ds(col, SC_LANES)] += sums[u]

        _token_stream(wid * per_worker, per_worker, group, chunks, idx_hbm, w_hbm, table_hbm, out_hbm,
                      idx_v, w_v, o_v, ring, sems, stage_sems, compute)

    return pl.kernel(
        body,
        out_type=jax.ShapeDtypeStruct((t, d), F32),
        mesh=_sc_mesh(),
        scratch_types=[
            pltpu.VMEM((2, group, hk), I32),
            pltpu.VMEM((2, group, hk), F32),
            pltpu.VMEM((group, d), F32),
            pltpu.VMEM((GATHER_BUFFERS, GATHER_ROWS, d), F32),
            pltpu.SemaphoreType.DMA((GATHER_BUFFERS,)),
            pltpu.SemaphoreType.DMA((2,)),
        ],
        compiler_params=pltpu.CompilerParams(needs_layout_passes=False),
        name="peer_mix",
    )(w, idx, table)


def _gate_body(d_ref, gate_ref, w_ref):
    x = d_ref[...]
    sqrt_half = np.sqrt(0.5).astype(np.float32)
    w_ref[...] = gate_ref[...] * (0.5 * x * (1.0 + lax.erf(x * sqrt_half)))


def _gate(dots, gate):
    t, hk = dots.shape
    tm = min(t, 2048)
    assert t % tm == 0
    spec = pl.BlockSpec((tm, hk), lambda i: (i, 0))
    return pl.pallas_call(
        _gate_body, grid=(t // tm,), in_specs=[spec, spec], out_specs=spec,
        out_shape=jax.ShapeDtypeStruct((t, hk), F32),
        compiler_params=pltpu.CompilerParams(dimension_semantics=("parallel",)),
        name="gate",
    )(dots, gate)


def _residual_body(y_all_ref, h_ref, p_ref, g_ref, y_ref, *, final):
    del y_all_ref
    x = h_ref[...] + p_ref[...]
    y_ref[...] = _rms(x, g_ref[...]) if final else x


def _residual(y_all, tok0, h2, p, g, final):
    t, d = h2.shape
    tm = min(t, 1024)
    assert t % tm == 0 and tok0 % tm == 0
    first = tok0 // tm
    spec = pl.BlockSpec((tm, d), lambda i: (i, 0))
    return pl.pallas_call(
        functools.partial(_residual_body, final=final),
        grid=(t // tm,),
        in_specs=[pl.BlockSpec(memory_space=pl.ANY), spec, spec, _const_spec((1, d))],
        out_specs=pl.BlockSpec((tm, d), lambda i: (first + i, 0)),
        out_shape=jax.ShapeDtypeStruct(y_all.shape, F32),
        input_output_aliases={0: 0},
        compiler_params=pltpu.CompilerParams(dimension_semantics=("parallel",)),
        name="residual",
    )(y_all, h2, p, g.reshape(1, d))


SEQ_PARTS = 8


def _num_parts(n, l):
    per_part = (n // SEQ_PARTS) * l
    ok = n % SEQ_PARTS == 0 and per_part % (SC_WORKERS * SUBLANES) == 0
    return SEQ_PARTS if ok else 1


def _trunk(x, hist_a, hist_b, norm_mix_g, w_in, conv_a_w, conv_b_w, conv_b_b, conv_norm_g, conv_norm_b,
           out_norm_a_g, out_norm_b_g, w_out, norm_ffn_g, w_q, sub_keys, u_experts, v_experts, final_norm_g):
    n, l, d = x.shape
    depth = w_in.shape[0]
    parts = _num_parts(n, l)
    seqs = n // parts
    new_a, new_b = [], []
    for layer in range(depth):
        last = layer == depth - 1
        y_all = jnp.zeros((n * l, d), F32)
        nas, nbs = [], []
        for q in range(parts):
            h, na, nb = _mixer(x, q * seqs, seqs, hist_a[layer], hist_b[layer], norm_mix_g[layer], w_in[layer],
                               conv_a_w[layer], conv_b_w[layer], conv_b_b[layer], conv_norm_g[layer],
                               conv_norm_b[layer], out_norm_a_g[layer], out_norm_b_g[layer], w_out[layer])
            h2 = h.reshape(seqs * l, d)
            xn, idx, gate = _route(h2, norm_ffn_g[layer], w_q[layer], sub_keys[layer])
            dots = _peer_dots(xn, idx, u_experts[layer])
            w = _gate(dots, gate)
            p = _peer_mix(w, idx, v_experts[layer])
            y_all = _residual(y_all, q * seqs * l, h2, p, final_norm_g if last else jnp.ones((d,), F32),
                              final=last)
            nas.append(na)
            nbs.append(nb)
        x = y_all.reshape(n, l, d)
        new_a.append(jnp.concatenate(nas, axis=0))
        new_b.append(jnp.concatenate(nbs, axis=0))
    return x, jnp.stack(new_a), jnp.stack(new_b)


def kernel(x_prompt, x_sample, cache_conv_a, cache_conv_b, norm_mix_g, w_in, conv_a_w, conv_b_w, conv_b_b,
           conv_norm_g, conv_norm_b, out_norm_a_g, out_norm_b_g, w_out, norm_ffn_g, w_q, sub_keys,
           u_experts, v_experts, final_norm_g):
    weights = (norm_mix_g, w_in, conv_a_w, conv_b_w, conv_b_b, conv_norm_g, conv_norm_b, out_norm_a_g,
               out_norm_b_g, w_out, norm_ffn_g, w_q, sub_keys, u_experts, v_experts, final_norm_g)
    depth = w_in.shape[0]
    n = x_prompt.shape[0]
    zeros_a = jnp.zeros((depth, n) + cache_conv_a.shape[2:], x_prompt.dtype)
    zeros_b = jnp.zeros((depth, n) + cache_conv_b.shape[2:], x_prompt.dtype)
    y_prompt, conv_a_prompt, conv_b_prompt = _trunk(x_prompt, zeros_a, zeros_b, *weights)
    y_sample, conv_a_sample, conv_b_sample = _trunk(x_sample, cache_conv_a, cache_conv_b, *weights)
    return (y_prompt, y_sample, conv_a_prompt, conv_b_prompt, conv_a_sample, conv_b_sample)
```

```python
import functools

import numpy as np
import jax
import jax.numpy as jnp
from jax import lax
from jax.experimental import pallas as pl
from jax.experimental.pallas import tpu as pltpu
from jax.experimental.pallas import tpu_sc as plsc

F32 = jnp.float32
I32 = jnp.int32
HIGHEST = lax.Precision.HIGHEST
EPS = 1e-6

LANES = 128
SUBLANES = 8
VMEM_LIMIT_BYTES = 56 * 1024 * 1024
SC_CORES = 2
SC_SUBCORES = 16
SC_LANES = 16
SC_WORKERS = SC_CORES * SC_SUBCORES

PEER_TOPK = 16


def _rms(x, g):
    return x * lax.rsqrt(jnp.mean(x * x, axis=-1, keepdims=True) + EPS) * g


def _const_spec(shape):
    zeros = (0,) * len(shape)
    return pl.BlockSpec(shape, lambda *_: zeros, pipeline_mode=pl.Buffered(1))


CONV_ROW_BLOCK = 64


def _mixer_body(x_ref, ha_ref, hb_ref, gmix_ref, win_ref, caw_ref, cbw_ref, cbb_ref, cng_ref, cnb_ref,
                ga_ref, gb_ref, wout_ref, h_ref, na_ref, nb_ref, xa_s, xb_s, cb_s,
                *, tm, wa, wb, ka, kb, pad_a, pad_b):
    i = pl.program_id(1)

    @pl.when(i == 0)
    def _():
        xa_s[0:pad_a] = ha_ref[...]
        xb_s[0:pad_b] = hb_ref[...]

    x = x_ref[...]
    xn = _rms(x, gmix_ref[...])
    proj = jnp.dot(xn, win_ref[...], precision=HIGHEST, preferred_element_type=F32)
    h_a = proj[:, 0:wa]
    c_a = proj[:, wa:2 * wa]
    b_a = proj[:, 2 * wa:3 * wa]
    v_b = proj[:, 3 * wa:3 * wa + wb]
    g_b = proj[:, 3 * wa + wb:3 * wa + 2 * wb]

    xa_s[pad_a:pad_a + tm] = c_a * h_a
    conv_a = caw_ref[0:1, :] * xa_s[pad_a - (ka - 1):pad_a - (ka - 1) + tm]
    for k in range(1, ka):
        lo = pad_a - (ka - 1) + k
        conv_a = conv_a + caw_ref[k:k + 1, :] * xa_s[lo:lo + tm]
    y_a = _rms(b_a * conv_a, ga_ref[...])

    xb_s[pad_b:pad_b + tm] = v_b * jax.nn.sigmoid(g_b)
    rb = min(tm, CONV_ROW_BLOCK)
    for r0 in range(0, tm, rb):
        base = r0 + pad_b - (kb - 1)
        acc = cbw_ref[0:1, :] * xb_s[base:base + rb]
        for k in range(1, kb):
            acc = acc + cbw_ref[k:k + 1, :] * xb_s[base + k:base + k + rb]
        cb_s[r0:r0 + rb] = acc + cbb_ref[...]
    cb = cb_s[...]
    mu = jnp.mean(cb, axis=-1, keepdims=True)
    xc = cb - mu
    ln = xc * lax.rsqrt(jnp.mean(xc * xc, axis=-1, keepdims=True) + EPS) * cng_ref[...] + cnb_ref[...]
    z = ln * jax.nn.sigmoid(ln)
    y_b = _rms(z, gb_ref[...])

    y = (jnp.dot(y_a, wout_ref[0:wa, :], precision=HIGHEST, preferred_element_type=F32)
         + jnp.dot(y_b, wout_ref[wa:wa + wb, :], precision=HIGHEST, preferred_element_type=F32))
    h_ref[...] = x + y

    xa_s[0:pad_a] = xa_s[tm:tm + pad_a]
    xb_s[0:pad_b] = xb_s[tm:tm + pad_b]

    @pl.when(i == pl.num_programs(1) - 1)
    def _():
        na_ref[...] = xa_s[0:pad_a]
        nb_ref[...] = xb_s[0:pad_b]


def _mixer(x, seq0, n, hist_a, hist_b, gmix, w_in, caw, cbw, cbb, cng, cnb, ga, gb, w_out):
    _, l, d = x.shape
    ka, wa = caw.shape
    kb, wb = cbw.shape
    pad_a = SUBLANES
    pad_b = -(-(kb - 1) // SUBLANES) * SUBLANES
    tm = min(l, 512)
    assert l % tm == 0 and tm % min(tm, CONV_ROW_BLOCK) == 0
    assert tm >= pad_b and tm % SUBLANES == 0 and ka - 1 <= pad_a
    ha = jnp.pad(hist_a, ((0, 0), (pad_a - (ka - 1), 0), (0, 0)))
    hb = jnp.pad(hist_b, ((0, 0), (pad_b - (kb - 1), 0), (0, 0)))
    row = lambda v: v.reshape(1, -1)
    body = functools.partial(_mixer_body, tm=tm, wa=wa, wb=wb, ka=ka, kb=kb, pad_a=pad_a, pad_b=pad_b)
    h, na, nb = pl.pallas_call(
        body,
        grid=(n, l // tm),
        in_specs=[
            pl.BlockSpec((None, tm, d), lambda b, i: (seq0 + b, i, 0)),
            pl.BlockSpec((None, pad_a, wa), lambda b, i: (seq0 + b, 0, 0)),
            pl.BlockSpec((None, pad_b, wb), lambda b, i: (seq0 + b, 0, 0)),
            _const_spec((1, d)),
            _const_spec(w_in.shape),
            _const_spec(caw.shape),
            _const_spec(cbw.shape),
            _const_spec((1, wb)),
            _const_spec((1, wb)),
            _const_spec((1, wb)),
            _const_spec((1, wa)),
            _const_spec((1, wb)),
            _const_spec(w_out.shape),
        ],
        out_specs=[
            pl.BlockSpec((None, tm, d), lambda b, i: (b, i, 0)),
            pl.BlockSpec((None, pad_a, wa), lambda b, i: (b, 0, 0)),
            pl.BlockSpec((None, pad_b, wb), lambda b, i: (b, 0, 0)),
        ],
        out_shape=[
            jax.ShapeDtypeStruct((n, l, d), F32),
            jax.ShapeDtypeStruct((n, pad_a, wa), F32),
            jax.ShapeDtypeStruct((n, pad_b, wb), F32),
        ],
        scratch_shapes=[
            pltpu.VMEM((pad_a + tm, wa), F32),
            pltpu.VMEM((pad_b + tm, wb), F32),
            pltpu.VMEM((tm, wb), F32),
        ],
        compiler_params=pltpu.CompilerParams(
            dimension_semantics=("parallel", "arbitrary"), vmem_limit_bytes=VMEM_LIMIT_BYTES),
        name="mixer",
    )(x, ha, hb, row(gmix), w_in, caw, cbw, row(cbb), row(cng), row(cnb), row(ga), row(gb), w_out)
    return h, na[:, pad_a - (ka - 1):], nb[:, pad_b - (kb - 1):]


def _top_rows(s, k):
    n, lanes = s.shape
    rows = lax.broadcasted_iota(I32, s.shape, 0)
    slot = lax.broadcasted_iota(I32, (k, lanes), 0)
    vals = jnp.zeros((k, lanes), F32)
    ids = jnp.zeros((k, lanes), I32)
    for it in range(k):
        m = jnp.max(s, axis=0, keepdims=True)
        am = jnp.min(jnp.where(s == m, rows, n), axis=0, keepdims=True)
        vals = jnp.where(slot == it, m, vals)
        ids = jnp.where(slot == it, am, ids)
        s = jnp.where(rows == am, -jnp.inf, s)
    return vals, ids


def _combine(v1, i1, v2, i2, n_keys):
    k, lanes = v1.shape
    half = k // 2
    vals, pos, eid = [], [], []

    def add(a_lo, a_n, b_lo, b_n):
        vals.append(v1[a_lo:a_lo + a_n] + v2[b_lo:b_lo + b_n])
        eid.append(i1[a_lo:a_lo + a_n] * n_keys + i2[b_lo:b_lo + b_n])
        r = lax.broadcasted_iota(I32, (max(a_n, b_n), lanes), 0)
        pos.append((a_lo + r) * k + b_lo if a_n > 1 else a_lo * k + b_lo + r)

    add(0, 1, 0, k)
    for a in range(1, half):
        add(a, 1, 0, half)
    add(half, k - half, 0, 1)
    cand = jnp.concatenate(vals, axis=0)
    pos = jnp.concatenate(pos, axis=0)
    eid = jnp.concatenate(eid, axis=0)

    slot = lax.broadcasted_iota(I32, (k, lanes), 0)
    top_s = jnp.zeros((k, lanes), F32)
    top_e = jnp.zeros((k, lanes), I32)
    for it in range(k):
        m = jnp.max(cand, axis=0, keepdims=True)
        sel = jnp.min(jnp.where(cand == m, pos, k * k), axis=0, keepdims=True)
        hit = pos == sel
        e = jnp.max(jnp.where(hit, eid, -1), axis=0, keepdims=True)
        top_s = jnp.where(slot == it, m, top_s)
        top_e = jnp.where(slot == it, e, top_e)
        cand = jnp.where(hit, -jnp.inf, cand)
    return top_s, top_e


def _route_body(h_ref, g_ref, wq_ref, sk_ref, xn_ref, idx_ref, gate_ref, q_s, st_s, et_s, gt_s,
                *, tm, heads, n_keys, d_key):
    hn = _rms(h_ref[...], g_ref[...])
    xn_ref[...] = hn
    q_s[...] = jnp.dot(hn, wq_ref[...], precision=HIGHEST, preferred_element_type=F32)
    for hp in range(2 * heads):
        st_s[hp] = lax.dot_general(sk_ref[hp], q_s[:, hp * d_key:(hp + 1) * d_key],
                                   (((1,), (1,)), ((), ())), precision=HIGHEST, preferred_element_type=F32)
    n_chunk = tm // LANES

    def per_head_chunk(j, carry):
        hd = j // n_chunk
        col = pl.multiple_of((j % n_chunk) * LANES, LANES)
        v1, i1 = _top_rows(st_s[2 * hd, :, pl.ds(col, LANES)], PEER_TOPK)
        v2, i2 = _top_rows(st_s[2 * hd + 1, :, pl.ds(col, LANES)], PEER_TOPK)
        top_s, top_e = _combine(v1, i1, v2, i2, n_keys)
        p = jnp.exp(top_s - top_s[0:1])
        gate = p / jnp.sum(p, axis=0, keepdims=True)
        row = pl.multiple_of(hd * PEER_TOPK, PEER_TOPK)
        et_s[pl.ds(row, PEER_TOPK), pl.ds(col, LANES)] = top_e
        gt_s[pl.ds(row, PEER_TOPK), pl.ds(col, LANES)] = gate
        return carry

    lax.fori_loop(0, heads * n_chunk, per_head_chunk, 0)
    idx_ref[...] = et_s[...].T
    gate_ref[...] = gt_s[...].T


def _route(h2, tok0, t, g, w_q, sub_keys):
    d = h2.shape[1]
    heads, _, n_keys, d_key = sub_keys.shape
    hk = heads * PEER_TOPK
    tm = min(t, 512)
    assert t % tm == 0 and tok0 % tm == 0 and tm % LANES == 0 and d_key == LANES and n_keys % SUBLANES == 0
    first = tok0 // tm
    sk = sub_keys.reshape(2 * heads, n_keys, d_key)
    body = functools.partial(_route_body, tm=tm, heads=heads, n_keys=n_keys, d_key=d_key)
    return pl.pallas_call(
        body,
        grid=(t // tm,),
        in_specs=[
            pl.BlockSpec((tm, d), lambda i: (first + i, 0)),
            _const_spec((1, d)),
            _const_spec(w_q.shape),
            _const_spec(sk.shape),
        ],
        out_specs=[
            pl.BlockSpec((tm, d), lambda i: (i, 0)),
            pl.BlockSpec((tm, hk), lambda i: (i, 0)),
            pl.BlockSpec((tm, hk), lambda i: (i, 0)),
        ],
        out_shape=[
            jax.ShapeDtypeStruct((t, d), F32),
            jax.ShapeDtypeStruct((t, hk), I32),
            jax.ShapeDtypeStruct((t, hk), F32),
        ],
        scratch_shapes=[
            pltpu.VMEM((tm, 2 * heads * d_key), F32),
            pltpu.VMEM((2 * heads, n_keys, tm), F32),
            pltpu.VMEM((hk, tm), I32),
            pltpu.VMEM((hk, tm), F32),
        ],
        compiler_params=pltpu.CompilerParams(
            dimension_semantics=("parallel",), vmem_limit_bytes=VMEM_LIMIT_BYTES),
        name="route",
    )(h2, g.reshape(1, d), w_q, sk)


GATHER_ROWS = 16
GATHER_BUFFERS = 4
LANE_STEPS = 4
BLOCK_LANES = LANE_STEPS * SC_LANES
BLOCKS_PER_TILE = LANES // BLOCK_LANES
GROUP_TOKENS = 16


def _sc_mesh():
    return plsc.VectorSubcoreMesh(core_axis_name="c", subcore_axis_name="s")


def _sc_geometry(t, hk):
    per_worker = t // SC_WORKERS
    group = min(per_worker, GROUP_TOKENS)
    chunks = hk // GATHER_ROWS
    assert t % SC_WORKERS == 0 and per_worker % group == 0 and group % SUBLANES == 0
    assert hk % GATHER_ROWS == 0 and chunks % GATHER_BUFFERS == 0 and GATHER_ROWS == SC_LANES
    return per_worker, group, chunks


def _token_stream(first_token, per_worker, group, chunks, idx_hbm, side_hbm, table_hbm, out_hbm,
                  idx_v, side_v, o_v, ring, sems, stage_sems, compute):
    n_groups = per_worker // group

    def stage(grp, par):
        tok0 = pl.multiple_of(first_token + grp * group, SUBLANES)
        return (pltpu.make_async_copy(idx_hbm.at[pl.ds(tok0, group)], idx_v.at[par], stage_sems.at[par]),
                pltpu.make_async_copy(side_hbm.at[pl.ds(tok0, group)], side_v.at[par], stage_sems.at[par]))

    def gather(par, g, c, slot):
        rows = idx_v[par, g, pl.ds(c * GATHER_ROWS, GATHER_ROWS)]
        return pltpu.make_async_copy(table_hbm.at[rows], ring.at[slot], sems.at[slot])

    for cp in stage(0, 0):
        cp.start()
    for cp in stage(0, 0):
        cp.wait()
    for c in range(GATHER_BUFFERS):
        gather(0, 0, c, c).start()

    @pl.loop(0, per_worker)
    def _(tk):
        grp = tk // group
        g = tk % group
        par = grp % 2
        more_groups = grp + 1 < n_groups

        @pl.when((g == 0) & more_groups)
        def _():
            for cp in stage(grp + 1, 1 - par):
                cp.start()

        @pl.when((g == group - 1) & more_groups)
        def _():
            for cp in stage(grp + 1, 1 - par):
                cp.wait()

        for c in range(chunks):
            slot = c % GATHER_BUFFERS
            gather(par, g, c, slot).wait()
            compute(par, g, c, slot)
            nxt = c + GATHER_BUFFERS
            if nxt < chunks:
                gather(par, g, nxt, slot).start()
            else:
                @pl.when(tk + 1 < per_worker)
                def _():
                    gather(((tk + 1) // group) % 2, (tk + 1) % group, nxt - chunks, slot).start()

        @pl.when(g == group - 1)
        def _():
            tok0 = pl.multiple_of(first_token + grp * group, SUBLANES)
            pltpu.sync_copy(o_v, out_hbm.at[pl.ds(tok0, group)])


def _row_tiles(table):
    e, d = table.shape
    assert d % LANES == 0
    return table.reshape(e, d // LANES, LANES)


def _peer_dots(xn, idx, table):
    t, d = xn.shape
    hk = idx.shape[1]
    per_worker, group, chunks = _sc_geometry(t, hk)
    tiles = table.shape[1]

    def body(xn_hbm, idx_hbm, table_hbm, out_hbm, idx_v, x_v, o_v, ring, sems, stage_sems):
        wid = lax.axis_index("c") * SC_SUBCORES + lax.axis_index("s")
        lane = lax.iota(I32, SC_LANES)

        def compute(par, g, c, slot):
            def col_block(cb, accs):
                tile, lane0 = cb // BLOCKS_PER_TILE, (cb % BLOCKS_PER_TILE) * BLOCK_LANES
                for u in range(LANE_STEPS):
                    xc = x_v[par, g, pl.ds(cb * BLOCK_LANES + u * SC_LANES, SC_LANES)]
                    accs = tuple(a + ring[slot, r, tile, pl.ds(lane0 + u * SC_LANES, SC_LANES)] * xc
                                 for r, a in enumerate(accs))
                return accs

            zero = jnp.zeros((SC_LANES,), F32)
            accs = lax.fori_loop(0, tiles * BLOCKS_PER_TILE, col_block, (zero,) * GATHER_ROWS)
            tot = zero
            for r in range(GATHER_ROWS):
                tot = jnp.where(lane == r, jnp.sum(accs[r]), tot)
            o_v[g, pl.ds(c * GATHER_ROWS, GATHER_ROWS)] = tot

        _token_stream(wid * per_worker, per_worker, group, chunks, idx_hbm, xn_hbm, table_hbm, out_hbm,
                      idx_v, x_v, o_v, ring, sems, stage_sems, compute)

    return pl.kernel(
        body,
        out_type=jax.ShapeDtypeStruct((t, hk), F32),
        mesh=_sc_mesh(),
        scratch_types=[
            pltpu.VMEM((2, group, hk), I32),
            pltpu.VMEM((2, group, d), F32),
            pltpu.VMEM((group, hk), F32),
            pltpu.VMEM((GATHER_BUFFERS, GATHER_ROWS, tiles, LANES), F32),
            pltpu.SemaphoreType.DMA((GATHER_BUFFERS,)),
            pltpu.SemaphoreType.DMA((2,)),
        ],
        compiler_params=pltpu.CompilerParams(needs_layout_passes=False),
        name="peer_dots",
    )(xn, idx, table)


def _peer_mix(w, idx, table):
    t, hk = w.shape
    tiles = table.shape[1]
    d = tiles * LANES
    per_worker, group, chunks = _sc_geometry(t, hk)

    def body(w_hbm, idx_hbm, table_hbm, out_hbm, idx_v, w_v, o_v, ring, sems, stage_sems):
        wid = lax.axis_index("c") * SC_SUBCORES + lax.axis_index("s")

        def compute(par, g, c, slot):
            wv = w_v[par, g, pl.ds(c * GATHER_ROWS, GATHER_ROWS)]
            ws = [jnp.full((SC_LANES,), wv[r], F32) for r in range(GATHER_ROWS)]

            @plsc.parallel_loop(0, tiles * BLOCKS_PER_TILE)
            def _(cb):
                tile, lane0 = cb // BLOCKS_PER_TILE, (cb % BLOCKS_PER_TILE) * BLOCK_LANES
                sums = []
                for u in range(LANE_STEPS):
                    part = [None] * 4
                    for r in range(GATHER_ROWS):
                        term = ws[r] * ring[slot, r, tile, pl.ds(lane0 + u * SC_LANES, SC_LANES)]
                        part[r % 4] = term if part[r % 4] is None else part[r % 4] + term
                    sums.append((part[0] + part[1]) + (part[2] + part[3]))
                for u in range(LANE_STEPS):
                    col = cb * BLOCK_LANES + u * SC_LANES
                    if c == 0:
                        o_v[g, pl.ds(col, SC_LANES)] = sums[u]
                    else:
                        plsc.addupdate(o_v.at[g, pl.ds(col, SC_LANES)], sums[u])

        _token_stream(wid * per_worker, per_worker, group, chunks, idx_hbm, w_hbm, table_hbm, out_hbm,
                      idx_v, w_v, o_v, ring, sems, stage_sems, compute)

    return pl.kernel(
        body,
        out_type=jax.ShapeDtypeStruct((t, d), F32),
        mesh=_sc_mesh(),
        scratch_types=[
            pltpu.VMEM((2, group, hk), I32),
            pltpu.VMEM((2, group, hk), F32),
            pltpu.VMEM((group, d), F32),
            pltpu.VMEM((GATHER_BUFFERS, GATHER_ROWS, tiles, LANES), F32),
            pltpu.SemaphoreType.DMA((GATHER_BUFFERS,)),
            pltpu.SemaphoreType.DMA((2,)),
        ],
        compiler_params=pltpu.CompilerParams(needs_layout_passes=False),
        name="peer_mix",
    )(w, idx, table)


def _gate_body(d_ref, gate_ref, w_ref):
    x = d_ref[...]
    sqrt_half = np.sqrt(0.5).astype(np.float32)
    w_ref[...] = gate_ref[...] * (0.5 * x * (1.0 + lax.erf(x * sqrt_half)))


def _gate(dots, gate):
    t, hk = dots.shape
    tm = min(t, 1024)
    assert t % tm == 0
    spec = pl.BlockSpec((tm, hk), lambda i: (i, 0))
    return pl.pallas_call(
        _gate_body, grid=(t // tm,), in_specs=[spec, spec], out_specs=spec,
        out_shape=jax.ShapeDtypeStruct((t, hk), F32),
        compiler_params=pltpu.CompilerParams(dimension_semantics=("parallel",)),
        name="gate",
    )(dots, gate)


def _residual_body(y_all_ref, h_ref, p_ref, g_ref, y_ref, *, final):
    del y_all_ref
    x = h_ref[...] + p_ref[...]
    y_ref[...] = _rms(x, g_ref[...]) if final else x


def _residual(y_all, y0, h2, h0, p, g, final):
    t, d = p.shape
    tm = min(t, 1024)
    assert t % tm == 0 and y0 % tm == 0 and h0 % tm == 0
    y_first, h_first = y0 // tm, h0 // tm
    return pl.pallas_call(
        functools.partial(_residual_body, final=final),
        grid=(t // tm,),
        in_specs=[pl.BlockSpec(memory_space=pl.ANY),
                  pl.BlockSpec((tm, d), lambda i: (h_first + i, 0)),
                  pl.BlockSpec((tm, d), lambda i: (i, 0)),
                  _const_spec((1, d))],
        out_specs=pl.BlockSpec((tm, d), lambda i: (y_first + i, 0)),
        out_shape=jax.ShapeDtypeStruct(y_all.shape, F32),
        input_output_aliases={0: 0},
        compiler_params=pltpu.CompilerParams(dimension_semantics=("parallel",)),
        name="residual",
    )(y_all, h2, p, g.reshape(1, d))


SEQ_PARTS = 8


def _num_parts(n, l):
    per_part = (n // SEQ_PARTS) * l
    ok = n % SEQ_PARTS == 0 and per_part % (SC_WORKERS * SUBLANES) == 0
    return SEQ_PARTS if ok else 1


FIRST_PIECE_TOKENS = 1024


def _token_pieces(q, tokens):
    quantum = SC_WORKERS * GROUP_TOKENS
    rest = tokens - FIRST_PIECE_TOKENS
    if q == 0 and rest > 0 and FIRST_PIECE_TOKENS % quantum == 0 and rest % quantum == 0:
        return [(0, FIRST_PIECE_TOKENS), (FIRST_PIECE_TOKENS, rest)]
    return [(0, tokens)]


def _trunk(x, hist_a, hist_b, norm_mix_g, w_in, conv_a_w, conv_b_w, conv_b_b, conv_norm_g, conv_norm_b,
           out_norm_a_g, out_norm_b_g, w_out, norm_ffn_g, w_q, sub_keys, u_tiles, v_tiles, final_norm_g):
    n, l, d = x.shape
    depth = w_in.shape[0]
    parts = _num_parts(n, l)
    seqs = n // parts
    new_a, new_b = [], []
    for layer in range(depth):
        last = layer == depth - 1
        y_all = jnp.zeros((n * l, d), F32)
        nas, nbs = [], []
        for q in range(parts):
            h, na, nb = _mixer(x, q * seqs, seqs, hist_a[layer], hist_b[layer], norm_mix_g[layer], w_in[layer],
                               conv_a_w[layer], conv_b_w[layer], conv_b_b[layer], conv_norm_g[layer],
                               conv_norm_b[layer], out_norm_a_g[layer], out_norm_b_g[layer], w_out[layer])
            h2 = h.reshape(seqs * l, d)
            for t0, tn in _token_pieces(q, seqs * l):
                xn, idx, gate = _route(h2, t0, tn, norm_ffn_g[layer], w_q[layer], sub_keys[layer])
                dots = _peer_dots(xn, idx, u_tiles[layer])
                w = _gate(dots, gate)
                p = _peer_mix(w, idx, v_tiles[layer])
                y_all = _residual(y_all, q * seqs * l + t0, h2, t0, p,
                                  final_norm_g if last else jnp.ones((d,), F32), final=last)
            nas.append(na)
            nbs.append(nb)
        x = y_all.reshape(n, l, d)
        new_a.append(jnp.concatenate(nas, axis=0))
        new_b.append(jnp.concatenate(nbs, axis=0))
    return x, jnp.stack(new_a), jnp.stack(new_b)


def kernel(x_prompt, x_sample, cache_conv_a, cache_conv_b, norm_mix_g, w_in, conv_a_w, conv_b_w, conv_b_b,
           conv_norm_g, conv_norm_b, out_norm_a_g, out_norm_b_g, w_out, norm_ffn_g, w_q, sub_keys,
           u_experts, v_experts, final_norm_g):
    depth = w_in.shape[0]
    u_tiles = [_row_tiles(u_experts[layer]) for layer in range(depth)]
    v_tiles = [_row_tiles(v_experts[layer]) for layer in range(depth)]
    weights = (norm_mix_g, w_in, conv_a_w, conv_b_w, conv_b_b, conv_norm_g, conv_norm_b, out_norm_a_g,
               out_norm_b_g, w_out, norm_ffn_g, w_q, sub_keys, u_tiles, v_tiles, final_norm_g)
    n = x_prompt.shape[0]
    y_sample, conv_a_sample, conv_b_sample = _trunk(x_sample, cache_conv_a, cache_conv_b, *weights)
    zeros_a = jnp.zeros((depth, n) + cache_conv_a.shape[2:], x_prompt.dtype)
    zeros_b = jnp.zeros((depth, n) + cache_conv_b.shape[2:], x_prompt.dtype)
    y_prompt, conv_a_prompt, conv_b_prompt = _trunk(x_prompt, zeros_a, zeros_b, *weights)
    return (y_prompt, y_sample, conv_a_prompt, conv_b_prompt, conv_a_sample, conv_b_sample)
```

```python
import functools

import numpy as np
import jax
import jax.numpy as jnp
from jax import lax
from jax.experimental import pallas as pl
from jax.experimental.pallas import tpu as pltpu
from jax.experimental.pallas import tpu_sc as plsc

F32 = jnp.float32
I32 = jnp.int32
HIGHEST = lax.Precision.HIGHEST
EPS = 1e-6

LANES = 128
SUBLANES = 8
VMEM_LIMIT_BYTES = 56 * 1024 * 1024
SC_CORES = 2
SC_SUBCORES = 16
SC_LANES = 16
SC_WORKERS = SC_CORES * SC_SUBCORES

PEER_TOPK = 16


def _rms(x, g):
    return x * lax.rsqrt(jnp.mean(x * x, axis=-1, keepdims=True) + EPS) * g


def _const_spec(shape):
    zeros = (0,) * len(shape)
    return pl.BlockSpec(shape, lambda *_: zeros, pipeline_mode=pl.Buffered(1))


CONV_ROW_BLOCK = 64


def _mixer_body(x_ref, ha_ref, hb_ref, gmix_ref, win_ref, caw_ref, cbw_ref, cbb_ref, cng_ref, cnb_ref,
                ga_ref, gb_ref, wout_ref, h_ref, na_ref, nb_ref, xa_s, xb_s, cb_s,
                *, tm, wa, wb, ka, kb, pad_a, pad_b):
    i = pl.program_id(1)

    @pl.when(i == 0)
    def _():
        xa_s[0:pad_a] = ha_ref[...]
        xb_s[0:pad_b] = hb_ref[...]

    x = x_ref[...]
    xn = _rms(x, gmix_ref[...])
    proj = jnp.dot(xn, win_ref[...], precision=HIGHEST, preferred_element_type=F32)
    h_a = proj[:, 0:wa]
    c_a = proj[:, wa:2 * wa]
    b_a = proj[:, 2 * wa:3 * wa]
    v_b = proj[:, 3 * wa:3 * wa + wb]
    g_b = proj[:, 3 * wa + wb:3 * wa + 2 * wb]

    xa_s[pad_a:pad_a + tm] = c_a * h_a
    conv_a = caw_ref[0:1, :] * xa_s[pad_a - (ka - 1):pad_a - (ka - 1) + tm]
    for k in range(1, ka):
        lo = pad_a - (ka - 1) + k
        conv_a = conv_a + caw_ref[k:k + 1, :] * xa_s[lo:lo + tm]
    y_a = _rms(b_a * conv_a, ga_ref[...])

    xb_s[pad_b:pad_b + tm] = v_b * jax.nn.sigmoid(g_b)
    rb = min(tm, CONV_ROW_BLOCK)
    for r0 in range(0, tm, rb):
        base = r0 + pad_b - (kb - 1)
        acc = cbw_ref[0:1, :] * xb_s[base:base + rb]
        for k in range(1, kb):
            acc = acc + cbw_ref[k:k + 1, :] * xb_s[base + k:base + k + rb]
        cb_s[r0:r0 + rb] = acc + cbb_ref[...]
    cb = cb_s[...]
    mu = jnp.mean(cb, axis=-1, keepdims=True)
    xc = cb - mu
    ln = xc * lax.rsqrt(jnp.mean(xc * xc, axis=-1, keepdims=True) + EPS) * cng_ref[...] + cnb_ref[...]
    z = ln * jax.nn.sigmoid(ln)
    y_b = _rms(z, gb_ref[...])

    y = (jnp.dot(y_a, wout_ref[0:wa, :], precision=HIGHEST, preferred_element_type=F32)
         + jnp.dot(y_b, wout_ref[wa:wa + wb, :], precision=HIGHEST, preferred_element_type=F32))
    h_ref[...] = x + y

    xa_s[0:pad_a] = xa_s[tm:tm + pad_a]
    xb_s[0:pad_b] = xb_s[tm:tm + pad_b]

    @pl.when(i == pl.num_programs(1) - 1)
    def _():
        na_ref[...] = xa_s[0:pad_a]
        nb_ref[...] = xb_s[0:pad_b]


def _mixer(x, seq0, n, hist_a, hist_b, gmix, w_in, caw, cbw, cbb, cng, cnb, ga, gb, w_out):
    _, l, d = x.shape
    ka, wa = caw.shape
    kb, wb = cbw.shape
    pad_a = SUBLANES
    pad_b = -(-(kb - 1) // SUBLANES) * SUBLANES
    tm = min(l, 512)
    assert l % tm == 0 and tm % min(tm, CONV_ROW_BLOCK) == 0
    assert tm >= pad_b and tm % SUBLANES == 0 and ka - 1 <= pad_a
    ha = jnp.pad(hist_a, ((0, 0), (pad_a - (ka - 1), 0), (0, 0)))
    hb = jnp.pad(hist_b, ((0, 0), (pad_b - (kb - 1), 0), (0, 0)))
    row = lambda v: v.reshape(1, -1)
    body = functools.partial(_mixer_body, tm=tm, wa=wa, wb=wb, ka=ka, kb=kb, pad_a=pad_a, pad_b=pad_b)
    h, na, nb = pl.pallas_call(
        body,
        grid=(n, l // tm),
        in_specs=[
            pl.BlockSpec((None, tm, d), lambda b, i: (seq0 + b, i, 0)),
            pl.BlockSpec((None, pad_a, wa), lambda b, i: (seq0 + b, 0, 0)),
            pl.BlockSpec((None, pad_b, wb), lambda b, i: (seq0 + b, 0, 0)),
            _const_spec((1, d)),
            _const_spec(w_in.shape),
            _const_spec(caw.shape),
            _const_spec(cbw.shape),
            _const_spec((1, wb)),
            _const_spec((1, wb)),
            _const_spec((1, wb)),
            _const_spec((1, wa)),
            _const_spec((1, wb)),
            _const_spec(w_out.shape),
        ],
        out_specs=[
            pl.BlockSpec((None, tm, d), lambda b, i: (b, i, 0)),
            pl.BlockSpec((None, pad_a, wa), lambda b, i: (b, 0, 0)),
            pl.BlockSpec((None, pad_b, wb), lambda b, i: (b, 0, 0)),
        ],
        out_shape=[
            jax.ShapeDtypeStruct((n, l, d), F32),
            jax.ShapeDtypeStruct((n, pad_a, wa), F32),
            jax.ShapeDtypeStruct((n, pad_b, wb), F32),
        ],
        scratch_shapes=[
            pltpu.VMEM((pad_a + tm, wa), F32),
            pltpu.VMEM((pad_b + tm, wb), F32),
            pltpu.VMEM((tm, wb), F32),
        ],
        compiler_params=pltpu.CompilerParams(
            dimension_semantics=("parallel", "arbitrary"), vmem_limit_bytes=VMEM_LIMIT_BYTES),
        name="mixer",
    )(x, ha, hb, row(gmix), w_in, caw, cbw, row(cbb), row(cng), row(cnb), row(ga), row(gb), w_out)
    return h, na[:, pad_a - (ka - 1):], nb[:, pad_b - (kb - 1):]


def _top_rows(s, k):
    n, lanes = s.shape
    rows = lax.broadcasted_iota(I32, s.shape, 0)
    slot = lax.broadcasted_iota(I32, (k, lanes), 0)
    vals = jnp.zeros((k, lanes), F32)
    ids = jnp.zeros((k, lanes), I32)
    for it in range(k):
        m = jnp.max(s, axis=0, keepdims=True)
        am = jnp.min(jnp.where(s == m, rows, n), axis=0, keepdims=True)
        vals = jnp.where(slot == it, m, vals)
        ids = jnp.where(slot == it, am, ids)
        s = jnp.where(rows == am, -jnp.inf, s)
    return vals, ids


def _combine(v1, i1, v2, i2, n_keys):
    k, lanes = v1.shape
    half = k // 2
    vals, pos, eid = [], [], []

    def add(a_lo, a_n, b_lo, b_n):
        vals.append(v1[a_lo:a_lo + a_n] + v2[b_lo:b_lo + b_n])
        eid.append(i1[a_lo:a_lo + a_n] * n_keys + i2[b_lo:b_lo + b_n])
        r = lax.broadcasted_iota(I32, (max(a_n, b_n), lanes), 0)
        pos.append((a_lo + r) * k + b_lo if a_n > 1 else a_lo * k + b_lo + r)

    add(0, 1, 0, k)
    for a in range(1, half):
        add(a, 1, 0, half)
    add(half, k - half, 0, 1)
    cand = jnp.concatenate(vals, axis=0)
    pos = jnp.concatenate(pos, axis=0)
    eid = jnp.concatenate(eid, axis=0)

    slot = lax.broadcasted_iota(I32, (k, lanes), 0)
    top_s = jnp.zeros((k, lanes), F32)
    top_e = jnp.zeros((k, lanes), I32)
    for it in range(k):
        m = jnp.max(cand, axis=0, keepdims=True)
        sel = jnp.min(jnp.where(cand == m, pos, k * k), axis=0, keepdims=True)
        hit = pos == sel
        e = jnp.max(jnp.where(hit, eid, -1), axis=0, keepdims=True)
        top_s = jnp.where(slot == it, m, top_s)
        top_e = jnp.where(slot == it, e, top_e)
        cand = jnp.where(hit, -jnp.inf, cand)
    return top_s, top_e


def _route_body(h_ref, g_ref, wq_ref, sk_ref, xn_ref, idx_ref, gate_ref, q_s, st_s, et_s, gt_s,
                *, tm, heads, n_keys, d_key):
    hn = _rms(h_ref[...], g_ref[...])
    xn_ref[...] = hn
    q_s[...] = jnp.dot(hn, wq_ref[...], precision=HIGHEST, preferred_element_type=F32)
    for hp in range(2 * heads):
        st_s[hp] = lax.dot_general(sk_ref[hp], q_s[:, hp * d_key:(hp + 1) * d_key],
                                   (((1,), (1,)), ((), ())), precision=HIGHEST, preferred_element_type=F32)
    n_chunk = tm // LANES

    def per_head_chunk(j, carry):
        hd = j // n_chunk
        col = pl.multiple_of((j % n_chunk) * LANES, LANES)
        v1, i1 = _top_rows(st_s[2 * hd, :, pl.ds(col, LANES)], PEER_TOPK)
        v2, i2 = _top_rows(st_s[2 * hd + 1, :, pl.ds(col, LANES)], PEER_TOPK)
        top_s, top_e = _combine(v1, i1, v2, i2, n_keys)
        p = jnp.exp(top_s - top_s[0:1])
        gate = p / jnp.sum(p, axis=0, keepdims=True)
        row = pl.multiple_of(hd * PEER_TOPK, PEER_TOPK)
        et_s[pl.ds(row, PEER_TOPK), pl.ds(col, LANES)] = top_e
        gt_s[pl.ds(row, PEER_TOPK), pl.ds(col, LANES)] = gate
        return carry

    lax.fori_loop(0, heads * n_chunk, per_head_chunk, 0)
    idx_ref[...] = et_s[...].T
    gate_ref[...] = gt_s[...].T


def _route(h2, tok0, t, g, w_q, sub_keys):
    d = h2.shape[1]
    heads, _, n_keys, d_key = sub_keys.shape
    hk = heads * PEER_TOPK
    tm = min(t, 512)
    assert t % tm == 0 and tok0 % tm == 0 and tm % LANES == 0 and d_key == LANES and n_keys % SUBLANES == 0
    first = tok0 // tm
    sk = sub_keys.reshape(2 * heads, n_keys, d_key)
    body = functools.partial(_route_body, tm=tm, heads=heads, n_keys=n_keys, d_key=d_key)
    return pl.pallas_call(
        body,
        grid=(t // tm,),
        in_specs=[
            pl.BlockSpec((tm, d), lambda i: (first + i, 0)),
            _const_spec((1, d)),
            _const_spec(w_q.shape),
            _const_spec(sk.shape),
        ],
        out_specs=[
            pl.BlockSpec((tm, d), lambda i: (i, 0)),
            pl.BlockSpec((tm, hk), lambda i: (i, 0)),
            pl.BlockSpec((tm, hk), lambda i: (i, 0)),
        ],
        out_shape=[
            jax.ShapeDtypeStruct((t, d), F32),
            jax.ShapeDtypeStruct((t, hk), I32),
            jax.ShapeDtypeStruct((t, hk), F32),
        ],
        scratch_shapes=[
            pltpu.VMEM((tm, 2 * heads * d_key), F32),
            pltpu.VMEM((2 * heads, n_keys, tm), F32),
            pltpu.VMEM((hk, tm), I32),
            pltpu.VMEM((hk, tm), F32),
        ],
        compiler_params=pltpu.CompilerParams(
            dimension_semantics=("parallel",), vmem_limit_bytes=VMEM_LIMIT_BYTES),
        name="route",
    )(h2, g.reshape(1, d), w_q, sk)


GATHER_ROWS = 16
GATHER_BUFFERS = 4
LANE_STEPS = 4
BLOCK_LANES = LANE_STEPS * SC_LANES
BLOCKS_PER_TILE = LANES // BLOCK_LANES
GROUP_TOKENS = 16


def _sc_mesh():
    return plsc.VectorSubcoreMesh(core_axis_name="c", subcore_axis_name="s")


def _sc_geometry(t, hk):
    per_worker = t // SC_WORKERS
    group = min(per_worker, GROUP_TOKENS)
    chunks = hk // GATHER_ROWS
    assert t % SC_WORKERS == 0 and per_worker % group == 0 and group % SUBLANES == 0
    assert hk % GATHER_ROWS == 0 and chunks % GATHER_BUFFERS == 0 and GATHER_ROWS == SC_LANES
    return per_worker, group, chunks


def _token_stream(first_token, per_worker, group, chunks, idx_hbm, side_hbm, table_hbm, out_hbm,
                  idx_v, side_v, o_v, ring, sems, stage_sems, compute):
    n_groups = per_worker // group

    def stage(grp, par):
        tok0 = pl.multiple_of(first_token + grp * group, SUBLANES)
        return (pltpu.make_async_copy(idx_hbm.at[pl.ds(tok0, group)], idx_v.at[par], stage_sems.at[par]),
                pltpu.make_async_copy(side_hbm.at[pl.ds(tok0, group)], side_v.at[par], stage_sems.at[par]))

    def gather(par, g, c, slot):
        rows = idx_v[par, g, pl.ds(c * GATHER_ROWS, GATHER_ROWS)]
        return pltpu.make_async_copy(table_hbm.at[rows], ring.at[slot], sems.at[slot])

    for cp in stage(0, 0):
        cp.start()
    for cp in stage(0, 0):
        cp.wait()
    for c in range(GATHER_BUFFERS):
        gather(0, 0, c, c).start()

    @pl.loop(0, per_worker)
    def _(tk):
        grp = tk // group
        g = tk % group
        par = grp % 2
        more_groups = grp + 1 < n_groups

        @pl.when((g == 0) & more_groups)
        def _():
            for cp in stage(grp + 1, 1 - par):
                cp.start()

        @pl.when((g == group - 1) & more_groups)
        def _():
            for cp in stage(grp + 1, 1 - par):
                cp.wait()

        for c in range(chunks):
            slot = c % GATHER_BUFFERS
            gather(par, g, c, slot).wait()
            compute(par, g, c, slot)
            nxt = c + GATHER_BUFFERS
            if nxt < chunks:
                gather(par, g, nxt, slot).start()
            else:
                @pl.when(tk + 1 < per_worker)
                def _():
                    gather(((tk + 1) // group) % 2, (tk + 1) % group, nxt - chunks, slot).start()

        @pl.when(g == group - 1)
        def _():
            tok0 = pl.multiple_of(first_token + grp * group, SUBLANES)
            pltpu.sync_copy(o_v, out_hbm.at[pl.ds(tok0, group)])


def _row_tiles(table):
    e, d = table.shape
    assert d % LANES == 0
    return table.reshape(e, d // LANES, LANES)


def _peer_dots(xn, idx, table, after):
    t, d = xn.shape
    hk = idx.shape[1]
    per_worker, group, chunks = _sc_geometry(t, hk)
    tiles = table.shape[1]

    def body(xn_hbm, idx_hbm, table_hbm, after_hbm, out_hbm, idx_v, x_v, o_v, ring, sems, stage_sems):
        del after_hbm
        wid = lax.axis_index("c") * SC_SUBCORES + lax.axis_index("s")
        lane = lax.iota(I32, SC_LANES)

        def compute(par, g, c, slot):
            def col_block(cb, accs):
                tile, lane0 = cb // BLOCKS_PER_TILE, (cb % BLOCKS_PER_TILE) * BLOCK_LANES
                for u in range(LANE_STEPS):
                    xc = x_v[par, g, pl.ds(cb * BLOCK_LANES + u * SC_LANES, SC_LANES)]
                    accs = tuple(a + ring[slot, r, tile, pl.ds(lane0 + u * SC_LANES, SC_LANES)] * xc
                                 for r, a in enumerate(accs))
                return accs

            zero = jnp.zeros((SC_LANES,), F32)
            accs = lax.fori_loop(0, tiles * BLOCKS_PER_TILE, col_block, (zero,) * GATHER_ROWS)
            tot = zero
            for r in range(GATHER_ROWS):
                tot = jnp.where(lane == r, jnp.sum(accs[r]), tot)
            o_v[g, pl.ds(c * GATHER_ROWS, GATHER_ROWS)] = tot

        _token_stream(wid * per_worker, per_worker, group, chunks, idx_hbm, xn_hbm, table_hbm, out_hbm,
                      idx_v, x_v, o_v, ring, sems, stage_sems, compute)

    return pl.kernel(
        body,
        out_type=jax.ShapeDtypeStruct((t, hk), F32),
        mesh=_sc_mesh(),
        scratch_types=[
            pltpu.VMEM((2, group, hk), I32),
            pltpu.VMEM((2, group, d), F32),
            pltpu.VMEM((group, hk), F32),
            pltpu.VMEM((GATHER_BUFFERS, GATHER_ROWS, tiles, LANES), F32),
            pltpu.SemaphoreType.DMA((GATHER_BUFFERS,)),
            pltpu.SemaphoreType.DMA((2,)),
        ],
        compiler_params=pltpu.CompilerParams(needs_layout_passes=False),
        name="peer_dots",
    )(xn, idx, table, after)


def _peer_mix(w, idx, table):
    t, hk = w.shape
    tiles = table.shape[1]
    d = tiles * LANES
    per_worker, group, chunks = _sc_geometry(t, hk)

    def body(w_hbm, idx_hbm, table_hbm, out_hbm, idx_v, w_v, o_v, ring, sems, stage_sems, acc_v):
        wid = lax.axis_index("c") * SC_SUBCORES + lax.axis_index("s")

        def compute(par, g, c, slot):
            wv = w_v[par, g, pl.ds(c * GATHER_ROWS, GATHER_ROWS)]
            ws = [jnp.full((SC_LANES,), wv[r], F32) for r in range(GATHER_ROWS)]

            @plsc.parallel_loop(0, tiles * BLOCKS_PER_TILE)
            def _(cb):
                tile, lane0 = cb // BLOCKS_PER_TILE, (cb % BLOCKS_PER_TILE) * BLOCK_LANES
                sums = []
                for u in range(LANE_STEPS):
                    part = [None] * 4
                    for r in range(GATHER_ROWS):
                        term = ws[r] * ring[slot, r, tile, pl.ds(lane0 + u * SC_LANES, SC_LANES)]
                        part[r % 4] = term if part[r % 4] is None else part[r % 4] + term
                    sums.append((part[0] + part[1]) + (part[2] + part[3]))
                for u in range(LANE_STEPS):
                    col = cb * BLOCK_LANES + u * SC_LANES
                    if c == 0:
                        acc_v[pl.ds(col, SC_LANES)] = sums[u]
                    elif c < chunks - 1:
                        acc_v[pl.ds(col, SC_LANES)] = acc_v[pl.ds(col, SC_LANES)] + sums[u]
                    else:
                        o_v[g, pl.ds(col, SC_LANES)] = acc_v[pl.ds(col, SC_LANES)] + sums[u]

        _token_stream(wid * per_worker, per_worker, group, chunks, idx_hbm, w_hbm, table_hbm, out_hbm,
                      idx_v, w_v, o_v, ring, sems, stage_sems, compute)

    return pl.kernel(
        body,
        out_type=jax.ShapeDtypeStruct((t, d), F32),
        mesh=_sc_mesh(),
        scratch_types=[
            pltpu.VMEM((2, group, hk), I32),
            pltpu.VMEM((2, group, hk), F32),
            pltpu.VMEM((group, d), F32),
            pltpu.VMEM((GATHER_BUFFERS, GATHER_ROWS, tiles, LANES), F32),
            pltpu.SemaphoreType.DMA((GATHER_BUFFERS,)),
            pltpu.SemaphoreType.DMA((2,)),
            pltpu.VMEM((d,), F32),
        ],
        compiler_params=pltpu.CompilerParams(needs_layout_passes=False),
        name="peer_mix",
    )(w, idx, table)


def _gate_body(d_ref, gate_ref, w_ref):
    x = d_ref[...]
    sqrt_half = np.sqrt(0.5).astype(np.float32)
    w_ref[...] = gate_ref[...] * (0.5 * x * (1.0 + lax.erf(x * sqrt_half)))


def _gate(dots, gate):
    t, hk = dots.shape
    tm = min(t, 1024)
    assert t % tm == 0
    spec = pl.BlockSpec((tm, hk), lambda i: (i, 0))
    return pl.pallas_call(
        _gate_body, grid=(t // tm,), in_specs=[spec, spec], out_specs=spec,
        out_shape=jax.ShapeDtypeStruct((t, hk), F32),
        compiler_params=pltpu.CompilerParams(dimension_semantics=("parallel",)),
        name="gate",
    )(dots, gate)


def _residual_body(y_all_ref, h_ref, p_ref, g_ref, y_ref, *, final):
    del y_all_ref
    x = h_ref[...] + p_ref[...]
    y_ref[...] = _rms(x, g_ref[...]) if final else x


def _residual(y_all, y0, h2, h0, p, g, final):
    t, d = p.shape
    tm = min(t, 1024)
    assert t % tm == 0 and y0 % tm == 0 and h0 % tm == 0
    y_first, h_first = y0 // tm, h0 // tm
    return pl.pallas_call(
        functools.partial(_residual_body, final=final),
        grid=(t // tm,),
        in_specs=[pl.BlockSpec(memory_space=pl.ANY),
                  pl.BlockSpec((tm, d), lambda i: (h_first + i, 0)),
                  pl.BlockSpec((tm, d), lambda i: (i, 0)),
                  _const_spec((1, d))],
        out_specs=pl.BlockSpec((tm, d), lambda i: (y_first + i, 0)),
        out_shape=jax.ShapeDtypeStruct(y_all.shape, F32),
        input_output_aliases={0: 0},
        compiler_params=pltpu.CompilerParams(dimension_semantics=("parallel",)),
        name="residual",
    )(y_all, h2, p, g.reshape(1, d))


SEQ_PARTS = 8


def _num_parts(n, l):
    per_part = (n // SEQ_PARTS) * l
    ok = n % SEQ_PARTS == 0 and per_part % (SC_WORKERS * SUBLANES) == 0
    return SEQ_PARTS if ok else 1


MIX_LAG = 2
FIRST_PIECE_TOKENS = 1024


def _token_pieces(q, tokens):
    quantum = SC_WORKERS * GROUP_TOKENS
    rest = tokens - FIRST_PIECE_TOKENS
    if q == 0 and rest > 0 and FIRST_PIECE_TOKENS % quantum == 0 and rest % quantum == 0:
        return [(0, FIRST_PIECE_TOKENS), (FIRST_PIECE_TOKENS, rest)]
    return [(0, tokens)]


def _trunk(x, hist_a, hist_b, norm_mix_g, w_in, conv_a_w, conv_b_w, conv_b_b, conv_norm_g, conv_norm_b,
           out_norm_a_g, out_norm_b_g, w_out, norm_ffn_g, w_q, sub_keys, u_tiles, v_tiles, final_norm_g):
    n, l, d = x.shape
    depth = w_in.shape[0]
    parts = _num_parts(n, l)
    seqs = n // parts
    new_a, new_b = [], []
    for layer in range(depth):
        last = layer == depth - 1
        y_all = jnp.zeros((n * l, d), F32)
        nas, nbs, mixed = [], [], []
        for q in range(parts):
            h, na, nb = _mixer(x, q * seqs, seqs, hist_a[layer], hist_b[layer], norm_mix_g[layer], w_in[layer],
                               conv_a_w[layer], conv_b_w[layer], conv_b_b[layer], conv_norm_g[layer],
                               conv_norm_b[layer], out_norm_a_g[layer], out_norm_b_g[layer], w_out[layer])
            h2 = h.reshape(seqs * l, d)
            for t0, tn in _token_pieces(q, seqs * l):
                xn, idx, gate = _route(h2, t0, tn, norm_ffn_g[layer], w_q[layer], sub_keys[layer])
                after = mixed[-MIX_LAG] if len(mixed) >= MIX_LAG else idx
                dots = _peer_dots(xn, idx, u_tiles[layer], after)
                w = _gate(dots, gate)
                p = _peer_mix(w, idx, v_tiles[layer])
                mixed.append(p)
                y_all = _residual(y_all, q * seqs * l + t0, h2, t0, p,
                                  final_norm_g if last else jnp.ones((d,), F32), final=last)
            nas.append(na)
            nbs.append(nb)
        x = y_all.reshape(n, l, d)
        new_a.append(jnp.concatenate(nas, axis=0))
        new_b.append(jnp.concatenate(nbs, axis=0))
    return x, jnp.stack(new_a), jnp.stack(new_b)


def kernel(x_prompt, x_sample, cache_conv_a, cache_conv_b, norm_mix_g, w_in, conv_a_w, conv_b_w, conv_b_b,
           conv_norm_g, conv_norm_b, out_norm_a_g, out_norm_b_g, w_out, norm_ffn_g, w_q, sub_keys,
           u_experts, v_experts, final_norm_g):
    depth = w_in.shape[0]
    u_tiles = [_row_tiles(u_experts[layer]) for layer in range(depth)]
    v_tiles = [_row_tiles(v_experts[layer]) for layer in range(depth)]
    weights = (norm_mix_g, w_in, conv_a_w, conv_b_w, conv_b_b, conv_norm_g, conv_norm_b, out_norm_a_g,
               out_norm_b_g, w_out, norm_ffn_g, w_q, sub_keys, u_tiles, v_tiles, final_norm_g)
    n = x_prompt.shape[0]
    y_sample, conv_a_sample, conv_b_sample = _trunk(x_sample, cache_conv_a, cache_conv_b, *weights)
    zeros_a = jnp.zeros((depth, n) + cache_conv_a.shape[2:], x_prompt.dtype)
    zeros_b = jnp.zeros((depth, n) + cache_conv_b.shape[2:], x_prompt.dtype)
    y_prompt, conv_a_prompt, conv_b_prompt = _trunk(x_prompt, zeros_a, zeros_b, *weights)
    return (y_prompt, y_sample, conv_a_prompt, conv_b_prompt, conv_a_sample, conv_b_sample)
```

```python
import functools

import numpy as np
import jax
import jax.numpy as jnp
from jax import lax
from jax.experimental import pallas as pl
from jax.experimental.pallas import tpu as pltpu
from jax.experimental.pallas import tpu_sc as plsc

F32 = jnp.float32
I32 = jnp.int32
BF16 = jnp.bfloat16
EPS = 1e-6

LANES = 128
SUBLANES = 8
VMEM_LIMIT_BYTES = 56 * 1024 * 1024
SC_CORES = 2
SC_SUBCORES = 16
SC_LANES = 16
SC_WORKERS = SC_CORES * SC_SUBCORES

PEER_TOPK = 16


def _rms(x, g):
    return x * lax.rsqrt(jnp.mean(x * x, axis=-1, keepdims=True) + EPS) * g


def _const_spec(shape):
    zeros = (0,) * len(shape)
    return pl.BlockSpec(shape, lambda *_: zeros, pipeline_mode=pl.Buffered(1))


def _hi_lo(x):
    hi = x.astype(BF16)
    return hi, (x - hi.astype(F32)).astype(BF16)


def _dot3(a, b_hi, b_lo, dims=(((1,), (0,)), ((), ()))):
    a_hi, a_lo = _hi_lo(a)
    dot = functools.partial(lax.dot_general, dimension_numbers=dims, preferred_element_type=F32)
    return (dot(a_hi, b_lo) + dot(a_lo, b_hi)) + dot(a_hi, b_hi)


def _split_body(w_ref, s_ref):
    hi, lo = _hi_lo(w_ref[...])
    s_ref[0] = hi
    s_ref[1] = lo


def _split(w):
    r, c = w.shape
    tm = min(r, 256)
    assert r % tm == 0
    return pl.pallas_call(
        _split_body, grid=(r // tm,),
        in_specs=[pl.BlockSpec((tm, c), lambda i: (i, 0))],
        out_specs=pl.BlockSpec((2, tm, c), lambda i: (0, i, 0)),
        out_shape=jax.ShapeDtypeStruct((2, r, c), BF16),
        compiler_params=pltpu.CompilerParams(dimension_semantics=("parallel",)),
        name="split",
    )(w)


CONV_ROW_BLOCK = 64


def _mixer_body(x_ref, ha_ref, hb_ref, gmix_ref, win_ref, caw_ref, cbw_ref, cbb_ref, cng_ref, cnb_ref,
                ga_ref, gb_ref, wout_ref, h_ref, na_ref, nb_ref, xa_s, xb_s, cb_s,
                *, tm, wa, wb, ka, kb, pad_a, pad_b):
    i = pl.program_id(1)

    @pl.when(i == 0)
    def _():
        xa_s[0:pad_a] = ha_ref[...]
        xb_s[0:pad_b] = hb_ref[...]

    x = x_ref[...]
    xn = _rms(x, gmix_ref[...])
    proj = _dot3(xn, win_ref[0], win_ref[1])
    h_a = proj[:, 0:wa]
    c_a = proj[:, wa:2 * wa]
    b_a = proj[:, 2 * wa:3 * wa]
    v_b = proj[:, 3 * wa:3 * wa + wb]
    g_b = proj[:, 3 * wa + wb:3 * wa + 2 * wb]

    xa_s[pad_a:pad_a + tm] = c_a * h_a
    conv_a = caw_ref[0:1, :] * xa_s[pad_a - (ka - 1):pad_a - (ka - 1) + tm]
    for k in range(1, ka):
        lo = pad_a - (ka - 1) + k
        conv_a = conv_a + caw_ref[k:k + 1, :] * xa_s[lo:lo + tm]
    y_a = _rms(b_a * conv_a, ga_ref[...])

    xb_s[pad_b:pad_b + tm] = v_b * jax.nn.sigmoid(g_b)
    rb = min(tm, CONV_ROW_BLOCK)
    for r0 in range(0, tm, rb):
        base = r0 + pad_b - (kb - 1)
        acc = cbw_ref[0:1, :] * xb_s[base:base + rb]
        for k in range(1, kb):
            acc = acc + cbw_ref[k:k + 1, :] * xb_s[base + k:base + k + rb]
        cb_s[r0:r0 + rb] = acc + cbb_ref[...]
    cb = cb_s[...]
    mu = jnp.mean(cb, axis=-1, keepdims=True)
    xc = cb - mu
    ln = xc * lax.rsqrt(jnp.mean(xc * xc, axis=-1, keepdims=True) + EPS) * cng_ref[...] + cnb_ref[...]
    z = ln * jax.nn.sigmoid(ln)
    y_b = _rms(z, gb_ref[...])

    y = (_dot3(y_a, wout_ref[0, 0:wa, :], wout_ref[1, 0:wa, :])
         + _dot3(y_b, wout_ref[0, wa:wa + wb, :], wout_ref[1, wa:wa + wb, :]))
    h_ref[...] = x + y

    xa_s[0:pad_a] = xa_s[tm:tm + pad_a]
    xb_s[0:pad_b] = xb_s[tm:tm + pad_b]

    @pl.when(i == pl.num_programs(1) - 1)
    def _():
        na_ref[...] = xa_s[0:pad_a]
        nb_ref[...] = xb_s[0:pad_b]


def _mixer(x, seq0, n, hist_a, hist_b, gmix, w_in, caw, cbw, cbb, cng, cnb, ga, gb, w_out):
    _, l, d = x.shape
    ka, wa = caw.shape
    kb, wb = cbw.shape
    pad_a = SUBLANES
    pad_b = -(-(kb - 1) // SUBLANES) * SUBLANES
    tm = min(l, 512)
    assert l % tm == 0 and tm % min(tm, CONV_ROW_BLOCK) == 0
    assert tm >= pad_b and tm % SUBLANES == 0 and ka - 1 <= pad_a
    ha = jnp.pad(hist_a, ((0, 0), (pad_a - (ka - 1), 0), (0, 0)))
    hb = jnp.pad(hist_b, ((0, 0), (pad_b - (kb - 1), 0), (0, 0)))
    row = lambda v: v.reshape(1, -1)
    body = functools.partial(_mixer_body, tm=tm, wa=wa, wb=wb, ka=ka, kb=kb, pad_a=pad_a, pad_b=pad_b)
    h, na, nb = pl.pallas_call(
        body,
        grid=(n, l // tm),
        in_specs=[
            pl.BlockSpec((None, tm, d), lambda b, i: (seq0 + b, i, 0)),
            pl.BlockSpec((None, pad_a, wa), lambda b, i: (seq0 + b, 0, 0)),
            pl.BlockSpec((None, pad_b, wb), lambda b, i: (seq0 + b, 0, 0)),
            _const_spec((1, d)),
            _const_spec(w_in.shape),
            _const_spec(caw.shape),
            _const_spec(cbw.shape),
            _const_spec((1, wb)),
            _const_spec((1, wb)),
            _const_spec((1, wb)),
            _const_spec((1, wa)),
            _const_spec((1, wb)),
            _const_spec(w_out.shape),
        ],
        out_specs=[
            pl.BlockSpec((None, tm, d), lambda b, i: (b, i, 0)),
            pl.BlockSpec((None, pad_a, wa), lambda b, i: (b, 0, 0)),
            pl.BlockSpec((None, pad_b, wb), lambda b, i: (b, 0, 0)),
        ],
        out_shape=[
            jax.ShapeDtypeStruct((n, l, d), F32),
            jax.ShapeDtypeStruct((n, pad_a, wa), F32),
            jax.ShapeDtypeStruct((n, pad_b, wb), F32),
        ],
        scratch_shapes=[
            pltpu.VMEM((pad_a + tm, wa), F32),
            pltpu.VMEM((pad_b + tm, wb), F32),
            pltpu.VMEM((tm, wb), F32),
        ],
        compiler_params=pltpu.CompilerParams(
            dimension_semantics=("parallel", "arbitrary"), vmem_limit_bytes=VMEM_LIMIT_BYTES),
        name="mixer",
    )(x, ha, hb, row(gmix), w_in, caw, cbw, row(cbb), row(cng), row(cnb), row(ga), row(gb), w_out)
    return h, na[:, pad_a - (ka - 1):], nb[:, pad_b - (kb - 1):]


def _top_rows(s, k):
    n, lanes = s.shape
    rows = lax.broadcasted_iota(I32, s.shape, 0)
    slot = lax.broadcasted_iota(I32, (k, lanes), 0)
    vals = jnp.zeros((k, lanes), F32)
    ids = jnp.zeros((k, lanes), I32)
    for it in range(k):
        m = jnp.max(s, axis=0, keepdims=True)
        am = jnp.min(jnp.where(s == m, rows, n), axis=0, keepdims=True)
        vals = jnp.where(slot == it, m, vals)
        ids = jnp.where(slot == it, am, ids)
        s = jnp.where(rows == am, -jnp.inf, s)
    return vals, ids


def _combine(v1, i1, v2, i2, n_keys):
    k, lanes = v1.shape
    half = k // 2
    vals, pos, eid = [], [], []

    def add(a_lo, a_n, b_lo, b_n):
        vals.append(v1[a_lo:a_lo + a_n] + v2[b_lo:b_lo + b_n])
        eid.append(i1[a_lo:a_lo + a_n] * n_keys + i2[b_lo:b_lo + b_n])
        r = lax.broadcasted_iota(I32, (max(a_n, b_n), lanes), 0)
        pos.append((a_lo + r) * k + b_lo if a_n > 1 else a_lo * k + b_lo + r)

    add(0, 1, 0, k)
    for a in range(1, half):
        add(a, 1, 0, half)
    add(half, k - half, 0, 1)
    cand = jnp.concatenate(vals, axis=0)
    pos = jnp.concatenate(pos, axis=0)
    eid = jnp.concatenate(eid, axis=0)

    slot = lax.broadcasted_iota(I32, (k, lanes), 0)
    top_s = jnp.zeros((k, lanes), F32)
    top_e = jnp.zeros((k, lanes), I32)
    for it in range(k):
        m = jnp.max(cand, axis=0, keepdims=True)
        sel = jnp.min(jnp.where(cand == m, pos, k * k), axis=0, keepdims=True)
        hit = pos == sel
        e = jnp.max(jnp.where(hit, eid, -1), axis=0, keepdims=True)
        top_s = jnp.where(slot == it, m, top_s)
        top_e = jnp.where(slot == it, e, top_e)
        cand = jnp.where(hit, -jnp.inf, cand)
    return top_s, top_e


def _route_body(h_ref, g_ref, wq_ref, sk_ref, xn_ref, idx_ref, gate_ref, qh_s, ql_s, st_s, et_s, gt_s,
                *, tm, heads, n_keys, d_key):
    hn = _rms(h_ref[...], g_ref[...])
    xn_ref[...] = hn
    qh_s[...], ql_s[...] = _hi_lo(_dot3(hn, wq_ref[0], wq_ref[1]))
    dot_nt = functools.partial(lax.dot_general, dimension_numbers=(((1,), (1,)), ((), ())),
                               preferred_element_type=F32)
    for hp in range(2 * heads):
        cols = slice(hp * d_key, (hp + 1) * d_key)
        st_s[hp] = ((dot_nt(sk_ref[0, hp], ql_s[:, cols]) + dot_nt(sk_ref[1, hp], qh_s[:, cols]))
                    + dot_nt(sk_ref[0, hp], qh_s[:, cols]))
    n_chunk = tm // LANES

    def per_head_chunk(j, carry):
        hd = j // n_chunk
        col = pl.multiple_of((j % n_chunk) * LANES, LANES)
        v1, i1 = _top_rows(st_s[2 * hd, :, pl.ds(col, LANES)], PEER_TOPK)
        v2, i2 = _top_rows(st_s[2 * hd + 1, :, pl.ds(col, LANES)], PEER_TOPK)
        top_s, top_e = _combine(v1, i1, v2, i2, n_keys)
        p = jnp.exp(top_s - top_s[0:1])
        gate = p / jnp.sum(p, axis=0, keepdims=True)
        row = pl.multiple_of(hd * PEER_TOPK, PEER_TOPK)
        et_s[pl.ds(row, PEER_TOPK), pl.ds(col, LANES)] = top_e
        gt_s[pl.ds(row, PEER_TOPK), pl.ds(col, LANES)] = gate
        return carry

    lax.fori_loop(0, heads * n_chunk, per_head_chunk, 0)
    idx_ref[...] = et_s[...].T
    gate_ref[...] = gt_s[...].T


def _route(h2, tok0, t, g, w_q, sk):
    d = h2.shape[1]
    _, heads2, n_keys, d_key = sk.shape
    heads = heads2 // 2
    hk = heads * PEER_TOPK
    tm = min(t, 512)
    assert t % tm == 0 and tok0 % tm == 0 and tm % LANES == 0 and d_key == LANES and n_keys % SUBLANES == 0
    first = tok0 // tm
    body = functools.partial(_route_body, tm=tm, heads=heads, n_keys=n_keys, d_key=d_key)
    return pl.pallas_call(
        body,
        grid=(t // tm,),
        in_specs=[
            pl.BlockSpec((tm, d), lambda i: (first + i, 0)),
            _const_spec((1, d)),
            _const_spec(w_q.shape),
            _const_spec(sk.shape),
        ],
        out_specs=[
            pl.BlockSpec((tm, d), lambda i: (i, 0)),
            pl.BlockSpec((tm, hk), lambda i: (i, 0)),
            pl.BlockSpec((tm, hk), lambda i: (i, 0)),
        ],
        out_shape=[
            jax.ShapeDtypeStruct((t, d), F32),
            jax.ShapeDtypeStruct((t, hk), I32),
            jax.ShapeDtypeStruct((t, hk), F32),
        ],
        scratch_shapes=[
            pltpu.VMEM((tm, 2 * heads * d_key), BF16),
            pltpu.VMEM((tm, 2 * heads * d_key), BF16),
            pltpu.VMEM((2 * heads, n_keys, tm), F32),
            pltpu.VMEM((hk, tm), I32),
            pltpu.VMEM((hk, tm), F32),
        ],
        compiler_params=pltpu.CompilerParams(
            dimension_semantics=("parallel",), vmem_limit_bytes=VMEM_LIMIT_BYTES),
        name="route",
    )(h2, g.reshape(1, d), w_q, sk)


GATHER_ROWS = 16
GATHER_BUFFERS = 4
LANE_STEPS = 4
BLOCK_LANES = LANE_STEPS * SC_LANES
BLOCKS_PER_TILE = LANES // BLOCK_LANES
GROUP_TOKENS = 16


def _sc_mesh():
    return plsc.VectorSubcoreMesh(core_axis_name="c", subcore_axis_name="s")


def _sc_geometry(t, hk):
    per_worker = t // SC_WORKERS
    group = min(per_worker, GROUP_TOKENS)
    chunks = hk // GATHER_ROWS
    assert t % SC_WORKERS == 0 and per_worker % group == 0 and group % SUBLANES == 0
    assert hk % GATHER_ROWS == 0 and chunks % GATHER_BUFFERS == 0 and GATHER_ROWS == SC_LANES
    return per_worker, group, chunks


def _token_stream(first_token, per_worker, group, chunks, idx_hbm, side_hbm, table_hbm, out_hbm,
                  idx_v, side_v, o_v, ring, sems, stage_sems, compute):
    n_groups = per_worker // group

    def stage(grp, par):
        tok0 = pl.multiple_of(first_token + grp * group, SUBLANES)
        return (pltpu.make_async_copy(idx_hbm.at[pl.ds(tok0, group)], idx_v.at[par], stage_sems.at[par]),
                pltpu.make_async_copy(side_hbm.at[pl.ds(tok0, group)], side_v.at[par], stage_sems.at[par]))

    def gather(par, g, c, slot):
        rows = idx_v[par, g, pl.ds(c * GATHER_ROWS, GATHER_ROWS)]
        return pltpu.make_async_copy(table_hbm.at[rows], ring.at[slot], sems.at[slot])

    for cp in stage(0, 0):
        cp.start()
    for cp in stage(0, 0):
        cp.wait()
    for c in range(GATHER_BUFFERS):
        gather(0, 0, c, c).start()

    @pl.loop(0, per_worker)
    def _(tk):
        grp = tk // group
        g = tk % group
        par = grp % 2
        more_groups = grp + 1 < n_groups

        @pl.when((g == 0) & more_groups)
        def _():
            for cp in stage(grp + 1, 1 - par):
                cp.start()

        @pl.when((g == group - 1) & more_groups)
        def _():
            for cp in stage(grp + 1, 1 - par):
                cp.wait()

        for c in range(chunks):
            slot = c % GATHER_BUFFERS
            gather(par, g, c, slot).wait()
            compute(par, g, c, slot)
            nxt = c + GATHER_BUFFERS
            if nxt < chunks:
                gather(par, g, nxt, slot).start()
            else:
                @pl.when(tk + 1 < per_worker)
                def _():
                    gather(((tk + 1) // group) % 2, (tk + 1) % group, nxt - chunks, slot).start()

        @pl.when(g == group - 1)
        def _():
            tok0 = pl.multiple_of(first_token + grp * group, SUBLANES)
            pltpu.sync_copy(o_v, out_hbm.at[pl.ds(tok0, group)])


def _row_tiles(table):
    e, d = table.shape
    assert d % LANES == 0
    return table.reshape(e, d // LANES, LANES)


def _peer_dots(xn, idx, table, after):
    t, d = xn.shape
    hk = idx.shape[1]
    per_worker, group, chunks = _sc_geometry(t, hk)
    tiles = table.shape[1]

    def body(xn_hbm, idx_hbm, table_hbm, after_hbm, out_hbm, idx_v, x_v, o_v, ring, sems, stage_sems):
        del after_hbm
        wid = lax.axis_index("c") * SC_SUBCORES + lax.axis_index("s")
        lane = lax.iota(I32, SC_LANES)

        def compute(par, g, c, slot):
            def col_block(cb, accs):
                tile, lane0 = cb // BLOCKS_PER_TILE, (cb % BLOCKS_PER_TILE) * BLOCK_LANES
                for u in range(LANE_STEPS):
                    xc = x_v[par, g, pl.ds(cb * BLOCK_LANES + u * SC_LANES, SC_LANES)]
                    accs = tuple(a + ring[slot, r, tile, pl.ds(lane0 + u * SC_LANES, SC_LANES)] * xc
                                 for r, a in enumerate(accs))
                return accs

            zero = jnp.zeros((SC_LANES,), F32)
            accs = lax.fori_loop(0, tiles * BLOCKS_PER_TILE, col_block, (zero,) * GATHER_ROWS)
            tot = zero
            for r in range(GATHER_ROWS):
                tot = jnp.where(lane == r, jnp.sum(accs[r]), tot)
            o_v[g, pl.ds(c * GATHER_ROWS, GATHER_ROWS)] = tot

        _token_stream(wid * per_worker, per_worker, group, chunks, idx_hbm, xn_hbm, table_hbm, out_hbm,
                      idx_v, x_v, o_v, ring, sems, stage_sems, compute)

    return pl.kernel(
        body,
        out_type=jax.ShapeDtypeStruct((t, hk), F32),
        mesh=_sc_mesh(),
        scratch_types=[
            pltpu.VMEM((2, group, hk), I32),
            pltpu.VMEM((2, group, d), F32),
            pltpu.VMEM((group, hk), F32),
            pltpu.VMEM((GATHER_BUFFERS, GATHER_ROWS, tiles, LANES), F32),
            pltpu.SemaphoreType.DMA((GATHER_BUFFERS,)),
            pltpu.SemaphoreType.DMA((2,)),
        ],
        compiler_params=pltpu.CompilerParams(needs_layout_passes=False),
        name="peer_dots",
    )(xn, idx, table, after)


def _peer_mix(w, idx, table):
    t, hk = w.shape
    tiles = table.shape[1]
    d = tiles * LANES
    per_worker, group, chunks = _sc_geometry(t, hk)

    def body(w_hbm, idx_hbm, table_hbm, out_hbm, idx_v, w_v, o_v, ring, sems, stage_sems, acc_v):
        wid = lax.axis_index("c") * SC_SUBCORES + lax.axis_index("s")

        def compute(par, g, c, slot):
            wv = w_v[par, g, pl.ds(c * GATHER_ROWS, GATHER_ROWS)]
            ws = [jnp.full((SC_LANES,), wv[r], F32) for r in range(GATHER_ROWS)]

            @plsc.parallel_loop(0, tiles * BLOCKS_PER_TILE)
            def _(cb):
                tile, lane0 = cb // BLOCKS_PER_TILE, (cb % BLOCKS_PER_TILE) * BLOCK_LANES
                for u in range(LANE_STEPS):
                    col = cb * BLOCK_LANES + u * SC_LANES
                    part = [None] * 4 if c == 0 else [acc_v[pl.ds(col, SC_LANES)], None, None, None]
                    for r in range(GATHER_ROWS):
                        term = ws[r] * ring[slot, r, tile, pl.ds(lane0 + u * SC_LANES, SC_LANES)]
                        part[r % 4] = term if part[r % 4] is None else part[r % 4] + term
                    total = (part[0] + part[1]) + (part[2] + part[3])
                    if c < chunks - 1:
                        acc_v[pl.ds(col, SC_LANES)] = total
                    else:
                        o_v[g, pl.ds(col, SC_LANES)] = total

        _token_stream(wid * per_worker, per_worker, group, chunks, idx_hbm, w_hbm, table_hbm, out_hbm,
                      idx_v, w_v, o_v, ring, sems, stage_sems, compute)

    return pl.kernel(
        body,
        out_type=jax.ShapeDtypeStruct((t, d), F32),
        mesh=_sc_mesh(),
        scratch_types=[
            pltpu.VMEM((2, group, hk), I32),
            pltpu.VMEM((2, group, hk), F32),
            pltpu.VMEM((group, d), F32),
            pltpu.VMEM((GATHER_BUFFERS, GATHER_ROWS, tiles, LANES), F32),
            pltpu.SemaphoreType.DMA((GATHER_BUFFERS,)),
            pltpu.SemaphoreType.DMA((2,)),
            pltpu.VMEM((d,), F32),
        ],
        compiler_params=pltpu.CompilerParams(needs_layout_passes=False),
        name="peer_mix",
    )(w, idx, table)


def _gate_body(d_ref, gate_ref, w_ref):
    x = d_ref[...]
    sqrt_half = np.sqrt(0.5).astype(np.float32)
    w_ref[...] = gate_ref[...] * (0.5 * x * (1.0 + lax.erf(x * sqrt_half)))


def _gate(dots, gate):
    t, hk = dots.shape
    tm = min(t, 1024)
    assert t % tm == 0
    spec = pl.BlockSpec((tm, hk), lambda i: (i, 0))
    return pl.pallas_call(
        _gate_body, grid=(t // tm,), in_specs=[spec, spec], out_specs=spec,
        out_shape=jax.ShapeDtypeStruct((t, hk), F32),
        compiler_params=pltpu.CompilerParams(dimension_semantics=("parallel",)),
        name="gate",
    )(dots, gate)


def _residual_body(y_all_ref, h_ref, p_ref, g_ref, y_ref, *, final):
    del y_all_ref
    x = h_ref[...] + p_ref[...]
    y_ref[...] = _rms(x, g_ref[...]) if final else x


def _residual(y_all, y0, h2, h0, p, g, final):
    t, d = p.shape
    tm = min(t, 1024)
    assert t % tm == 0 and y0 % tm == 0 and h0 % tm == 0
    y_first, h_first = y0 // tm, h0 // tm
    return pl.pallas_call(
        functools.partial(_residual_body, final=final),
        grid=(t // tm,),
        in_specs=[pl.BlockSpec(memory_space=pl.ANY),
                  pl.BlockSpec((tm, d), lambda i: (h_first + i, 0)),
                  pl.BlockSpec((tm, d), lambda i: (i, 0)),
                  _const_spec((1, d))],
        out_specs=pl.BlockSpec((tm, d), lambda i: (y_first + i, 0)),
        out_shape=jax.ShapeDtypeStruct(y_all.shape, F32),
        input_output_aliases={0: 0},
        compiler_params=pltpu.CompilerParams(dimension_semantics=("parallel",)),
        name="residual",
    )(y_all, h2, p, g.reshape(1, d))


SEQ_PARTS = 8


def _num_parts(n, l):
    per_part = (n // SEQ_PARTS) * l
    ok = n % SEQ_PARTS == 0 and per_part % (SC_WORKERS * SUBLANES) == 0
    return SEQ_PARTS if ok else 1


MIX_LAG = 2
FIRST_PIECE_TOKENS = 1024


def _token_pieces(q, tokens):
    quantum = SC_WORKERS * GROUP_TOKENS
    rest = tokens - FIRST_PIECE_TOKENS
    if q == 0 and rest > 0 and FIRST_PIECE_TOKENS % quantum == 0 and rest % quantum == 0:
        return [(0, FIRST_PIECE_TOKENS), (FIRST_PIECE_TOKENS, rest)]
    return [(0, tokens)]


def _trunk(x, hist_a, hist_b, norm_mix_g, w_in, conv_a_w, conv_b_w, conv_b_b, conv_norm_g, conv_norm_b,
           out_norm_a_g, out_norm_b_g, w_out, norm_ffn_g, w_q, sub_keys, u_tiles, v_tiles, final_norm_g):
    n, l, d = x.shape
    depth = len(w_in)
    parts = _num_parts(n, l)
    seqs = n // parts
    new_a, new_b = [], []
    for layer in range(depth):
        last = layer == depth - 1
        y_all = jnp.zeros((n * l, d), F32)
        nas, nbs, mixed = [], [], []
        for q in range(parts):
            h, na, nb = _mixer(x, q * seqs, seqs, hist_a[layer], hist_b[layer], norm_mix_g[layer], w_in[layer],
                               conv_a_w[layer], conv_b_w[layer], conv_b_b[layer], conv_norm_g[layer],
                               conv_norm_b[layer], out_norm_a_g[layer], out_norm_b_g[layer], w_out[layer])
            h2 = h.reshape(seqs * l, d)
            for t0, tn in _token_pieces(q, seqs * l):
                xn, idx, gate = _route(h2, t0, tn, norm_ffn_g[layer], w_q[layer], sub_keys[layer])
                after = mixed[-MIX_LAG] if len(mixed) >= MIX_LAG else idx
                dots = _peer_dots(xn, idx, u_tiles[layer], after)
                w = _gate(dots, gate)
                p = _peer_mix(w, idx, v_tiles[layer])
                mixed.append(p)
                y_all = _residual(y_all, q * seqs * l + t0, h2, t0, p,
                                  final_norm_g if last else jnp.ones((d,), F32), final=last)
            nas.append(na)
            nbs.append(nb)
        x = y_all.reshape(n, l, d)
        new_a.append(jnp.concatenate(nas, axis=0))
        new_b.append(jnp.concatenate(nbs, axis=0))
    return x, jnp.stack(new_a), jnp.stack(new_b)


def kernel(x_prompt, x_sample, cache_conv_a, cache_conv_b, norm_mix_g, w_in, conv_a_w, conv_b_w, conv_b_b,
           conv_norm_g, conv_norm_b, out_norm_a_g, out_norm_b_g, w_out, norm_ffn_g, w_q, sub_keys,
           u_experts, v_experts, final_norm_g):
    depth = w_in.shape[0]
    _, heads, _, n_keys, d_key = sub_keys.shape
    u_tiles = [_row_tiles(u_experts[layer]) for layer in range(depth)]
    v_tiles = [_row_tiles(v_experts[layer]) for layer in range(depth)]
    w_in_s = [_split(w_in[layer]) for layer in range(depth)]
    w_out_s = [_split(w_out[layer]) for layer in range(depth)]
    w_q_s = [_split(w_q[layer]) for layer in range(depth)]
    sk_s = [_split(sub_keys[layer].reshape(heads * 2 * n_keys, d_key)).reshape(2, heads * 2, n_keys, d_key)
            for layer in range(depth)]
    weights = (norm_mix_g, w_in_s, conv_a_w, conv_b_w, conv_b_b, conv_norm_g, conv_norm_b, out_norm_a_g,
               out_norm_b_g, w_out_s, norm_ffn_g, w_q_s, sk_s, u_tiles, v_tiles, final_norm_g)
    n = x_prompt.shape[0]
    y_sample, conv_a_sample, conv_b_sample = _trunk(x_sample, cache_conv_a, cache_conv_b, *weights)
    zeros_a = jnp.zeros((depth, n) + cache_conv_a.shape[2:], x_prompt.dtype)
    zeros_b = jnp.zeros((depth, n) + cache_conv_b.shape[2:], x_prompt.dtype)
    y_prompt, conv_a_prompt, conv_b_prompt = _trunk(x_prompt, zeros_a, zeros_b, *weights)
    return (y_prompt, y_sample, conv_a_prompt, conv_b_prompt, conv_a_sample, conv_b_sample)
```

```python
import functools

import numpy as np
import jax
import jax.numpy as jnp
from jax import lax
from jax.experimental import pallas as pl
from jax.experimental.pallas import tpu as pltpu
from jax.experimental.pallas import tpu_sc as plsc

F32 = jnp.float32
I32 = jnp.int32
BF16 = jnp.bfloat16
EPS = 1e-6

LANES = 128
SUBLANES = 8
VMEM_LIMIT_BYTES = 56 * 1024 * 1024
SC_CORES = 2
SC_SUBCORES = 16
SC_LANES = 16
SC_WORKERS = SC_CORES * SC_SUBCORES

PEER_TOPK = 16


def _rms(x, g):
    return x * lax.rsqrt(jnp.mean(x * x, axis=-1, keepdims=True) + EPS) * g


def _const_spec(shape):
    zeros = (0,) * len(shape)
    return pl.BlockSpec(shape, lambda *_: zeros, pipeline_mode=pl.Buffered(1))


def _hi_lo(x):
    hi = x.astype(BF16)
    return hi, (x - hi.astype(F32)).astype(BF16)


def _dot3(a, b_hi, b_lo, dims=(((1,), (0,)), ((), ()))):
    a_hi, a_lo = _hi_lo(a)
    dot = functools.partial(lax.dot_general, dimension_numbers=dims, preferred_element_type=F32)
    return (dot(a_hi, b_lo) + dot(a_lo, b_hi)) + dot(a_hi, b_hi)


def _split_body(w_ref, s_ref):
    hi, lo = _hi_lo(w_ref[...])
    s_ref[0] = hi
    s_ref[1] = lo


def _split(w):
    r, c = w.shape
    tm = min(r, 256)
    assert r % tm == 0
    return pl.pallas_call(
        _split_body, grid=(r // tm,),
        in_specs=[pl.BlockSpec((tm, c), lambda i: (i, 0))],
        out_specs=pl.BlockSpec((2, tm, c), lambda i: (0, i, 0)),
        out_shape=jax.ShapeDtypeStruct((2, r, c), BF16),
        compiler_params=pltpu.CompilerParams(dimension_semantics=("parallel",)),
        name="split",
    )(w)


CONV_ROW_BLOCK = 64


def _mixer_body(x_ref, ha_ref, hb_ref, gmix_ref, win_ref, caw_ref, cbw_ref, cbb_ref, cng_ref, cnb_ref,
                ga_ref, gb_ref, wout_ref, h_ref, na_ref, nb_ref, xa_s, xb_s, cb_s,
                *, tm, wa, wb, ka, kb, pad_a, pad_b):
    i = pl.program_id(1)

    @pl.when(i == 0)
    def _():
        xa_s[0:pad_a] = ha_ref[...]
        xb_s[0:pad_b] = hb_ref[...]

    x = x_ref[...]
    xn = _rms(x, gmix_ref[...])
    proj = _dot3(xn, win_ref[0], win_ref[1])
    h_a = proj[:, 0:wa]
    c_a = proj[:, wa:2 * wa]
    b_a = proj[:, 2 * wa:3 * wa]
    v_b = proj[:, 3 * wa:3 * wa + wb]
    g_b = proj[:, 3 * wa + wb:3 * wa + 2 * wb]

    xa_s[pad_a:pad_a + tm] = c_a * h_a
    conv_a = caw_ref[0:1, :] * xa_s[pad_a - (ka - 1):pad_a - (ka - 1) + tm]
    for k in range(1, ka):
        lo = pad_a - (ka - 1) + k
        conv_a = conv_a + caw_ref[k:k + 1, :] * xa_s[lo:lo + tm]
    y_a = _rms(b_a * conv_a, ga_ref[...])

    xb_s[pad_b:pad_b + tm] = v_b * jax.nn.sigmoid(g_b)
    rb = min(tm, CONV_ROW_BLOCK)
    for r0 in range(0, tm, rb):
        base = r0 + pad_b - (kb - 1)
        acc = cbw_ref[0:1, :] * xb_s[base:base + rb]
        for k in range(1, kb):
            acc = acc + cbw_ref[k:k + 1, :] * xb_s[base + k:base + k + rb]
        cb_s[r0:r0 + rb] = acc + cbb_ref[...]
    cb = cb_s[...]
    mu = jnp.mean(cb, axis=-1, keepdims=True)
    xc = cb - mu
    ln = xc * lax.rsqrt(jnp.mean(xc * xc, axis=-1, keepdims=True) + EPS) * cng_ref[...] + cnb_ref[...]
    z = ln * jax.nn.sigmoid(ln)
    y_b = _rms(z, gb_ref[...])

    y = (_dot3(y_a, wout_ref[0, 0:wa, :], wout_ref[1, 0:wa, :])
         + _dot3(y_b, wout_ref[0, wa:wa + wb, :], wout_ref[1, wa:wa + wb, :]))
    h_ref[...] = x + y

    xa_s[0:pad_a] = xa_s[tm:tm + pad_a]
    xb_s[0:pad_b] = xb_s[tm:tm + pad_b]

    @pl.when(i == pl.num_programs(1) - 1)
    def _():
        na_ref[...] = xa_s[0:pad_a]
        nb_ref[...] = xb_s[0:pad_b]


def _mixer(x, seq0, n, hist_a, hist_b, gmix, w_in, caw, cbw, cbb, cng, cnb, ga, gb, w_out):
    _, l, d = x.shape
    ka, wa = caw.shape
    kb, wb = cbw.shape
    pad_a = SUBLANES
    pad_b = -(-(kb - 1) // SUBLANES) * SUBLANES
    tm = min(l, 512)
    assert l % tm == 0 and tm % min(tm, CONV_ROW_BLOCK) == 0
    assert tm >= pad_b and tm % SUBLANES == 0 and ka - 1 <= pad_a
    ha = jnp.pad(hist_a, ((0, 0), (pad_a - (ka - 1), 0), (0, 0)))
    hb = jnp.pad(hist_b, ((0, 0), (pad_b - (kb - 1), 0), (0, 0)))
    row = lambda v: v.reshape(1, -1)
    body = functools.partial(_mixer_body, tm=tm, wa=wa, wb=wb, ka=ka, kb=kb, pad_a=pad_a, pad_b=pad_b)
    h, na, nb = pl.pallas_call(
        body,
        grid=(n, l // tm),
        in_specs=[
            pl.BlockSpec((None, tm, d), lambda b, i: (seq0 + b, i, 0)),
            pl.BlockSpec((None, pad_a, wa), lambda b, i: (seq0 + b, 0, 0)),
            pl.BlockSpec((None, pad_b, wb), lambda b, i: (seq0 + b, 0, 0)),
            _const_spec((1, d)),
            _const_spec(w_in.shape),
            _const_spec(caw.shape),
            _const_spec(cbw.shape),
            _const_spec((1, wb)),
            _const_spec((1, wb)),
            _const_spec((1, wb)),
            _const_spec((1, wa)),
            _const_spec((1, wb)),
            _const_spec(w_out.shape),
        ],
        out_specs=[
            pl.BlockSpec((None, tm, d), lambda b, i: (b, i, 0)),
            pl.BlockSpec((None, pad_a, wa), lambda b, i: (b, 0, 0)),
            pl.BlockSpec((None, pad_b, wb), lambda b, i: (b, 0, 0)),
        ],
        out_shape=[
            jax.ShapeDtypeStruct((n, l, d), F32),
            jax.ShapeDtypeStruct((n, pad_a, wa), F32),
            jax.ShapeDtypeStruct((n, pad_b, wb), F32),
        ],
        scratch_shapes=[
            pltpu.VMEM((pad_a + tm, wa), F32),
            pltpu.VMEM((pad_b + tm, wb), F32),
            pltpu.VMEM((tm, wb), F32),
        ],
        compiler_params=pltpu.CompilerParams(
            dimension_semantics=("parallel", "arbitrary"), vmem_limit_bytes=VMEM_LIMIT_BYTES),
        name="mixer",
    )(x, ha, hb, row(gmix), w_in, caw, cbw, row(cbb), row(cng), row(cnb), row(ga), row(gb), w_out)
    return h, na[:, pad_a - (ka - 1):], nb[:, pad_b - (kb - 1):]


def _top_rows(s, k):
    n, lanes = s.shape
    rows = lax.broadcasted_iota(I32, s.shape, 0)
    slot = lax.broadcasted_iota(I32, (k, lanes), 0)
    vals = jnp.zeros((k, lanes), F32)
    ids = jnp.zeros((k, lanes), I32)
    for it in range(k):
        m = jnp.max(s, axis=0, keepdims=True)
        am = jnp.min(jnp.where(s == m, rows, n), axis=0, keepdims=True)
        vals = jnp.where(slot == it, m, vals)
        ids = jnp.where(slot == it, am, ids)
        s = jnp.where(rows == am, -jnp.inf, s)
    return vals, ids


def _combine(v1, i1, v2, i2, n_keys):
    k, lanes = v1.shape
    half = k // 2
    vals, pos, eid = [], [], []

    def add(a_lo, a_n, b_lo, b_n):
        vals.append(v1[a_lo:a_lo + a_n] + v2[b_lo:b_lo + b_n])
        eid.append(i1[a_lo:a_lo + a_n] * n_keys + i2[b_lo:b_lo + b_n])
        r = lax.broadcasted_iota(I32, (max(a_n, b_n), lanes), 0)
        pos.append((a_lo + r) * k + b_lo if a_n > 1 else a_lo * k + b_lo + r)

    add(0, 1, 0, k)
    for a in range(1, half):
        add(a, 1, 0, half)
    add(half, k - half, 0, 1)
    cand = jnp.concatenate(vals, axis=0)
    pos = jnp.concatenate(pos, axis=0)
    eid = jnp.concatenate(eid, axis=0)

    slot = lax.broadcasted_iota(I32, (k, lanes), 0)
    top_s = jnp.zeros((k, lanes), F32)
    top_e = jnp.zeros((k, lanes), I32)
    for it in range(k):
        m = jnp.max(cand, axis=0, keepdims=True)
        sel = jnp.min(jnp.where(cand == m, pos, k * k), axis=0, keepdims=True)
        hit = pos == sel
        e = jnp.max(jnp.where(hit, eid, -1), axis=0, keepdims=True)
        top_s = jnp.where(slot == it, m, top_s)
        top_e = jnp.where(slot == it, e, top_e)
        cand = jnp.where(hit, -jnp.inf, cand)
    return top_s, top_e


def _route_body(h_ref, g_ref, wq_ref, sk_ref, xn_ref, idx_ref, gate_ref, qh_s, ql_s, st_s, et_s, gt_s,
                *, tm, heads, n_keys, d_key):
    hn = _rms(h_ref[...], g_ref[...])
    xn_ref[...] = hn
    qh_s[...], ql_s[...] = _hi_lo(_dot3(hn, wq_ref[0], wq_ref[1]))
    dot_nt = functools.partial(lax.dot_general, dimension_numbers=(((1,), (1,)), ((), ())),
                               preferred_element_type=F32)
    for hp in range(2 * heads):
        cols = slice(hp * d_key, (hp + 1) * d_key)
        st_s[hp] = ((dot_nt(sk_ref[0, hp], ql_s[:, cols]) + dot_nt(sk_ref[1, hp], qh_s[:, cols]))
                    + dot_nt(sk_ref[0, hp], qh_s[:, cols]))
    n_chunk = tm // LANES

    def per_head_chunk(j, carry):
        hd = j // n_chunk
        col = pl.multiple_of((j % n_chunk) * LANES, LANES)
        v1, i1 = _top_rows(st_s[2 * hd, :, pl.ds(col, LANES)], PEER_TOPK)
        v2, i2 = _top_rows(st_s[2 * hd + 1, :, pl.ds(col, LANES)], PEER_TOPK)
        top_s, top_e = _combine(v1, i1, v2, i2, n_keys)
        p = jnp.exp(top_s - top_s[0:1])
        gate = p / jnp.sum(p, axis=0, keepdims=True)
        row = pl.multiple_of(hd * PEER_TOPK, PEER_TOPK)
        et_s[pl.ds(row, PEER_TOPK), pl.ds(col, LANES)] = top_e
        gt_s[pl.ds(row, PEER_TOPK), pl.ds(col, LANES)] = gate
        return carry

    lax.fori_loop(0, heads * n_chunk, per_head_chunk, 0)
    idx_ref[...] = et_s[...].T
    gate_ref[...] = gt_s[...].T


def _route(h2, tok0, t, g, w_q, sk):
    d = h2.shape[1]
    _, heads2, n_keys, d_key = sk.shape
    heads = heads2 // 2
    hk = heads * PEER_TOPK
    tm = min(t, 512)
    assert t % tm == 0 and tok0 % tm == 0 and tm % LANES == 0 and d_key == LANES and n_keys % SUBLANES == 0
    first = tok0 // tm
    body = functools.partial(_route_body, tm=tm, heads=heads, n_keys=n_keys, d_key=d_key)
    return pl.pallas_call(
        body,
        grid=(t // tm,),
        in_specs=[
            pl.BlockSpec((tm, d), lambda i: (first + i, 0)),
            _const_spec((1, d)),
            _const_spec(w_q.shape),
            _const_spec(sk.shape),
        ],
        out_specs=[
            pl.BlockSpec((tm, d), lambda i: (i, 0)),
            pl.BlockSpec((tm, hk), lambda i: (i, 0)),
            pl.BlockSpec((tm, hk), lambda i: (i, 0)),
        ],
        out_shape=[
            jax.ShapeDtypeStruct((t, d), F32),
            jax.ShapeDtypeStruct((t, hk), I32),
            jax.ShapeDtypeStruct((t, hk), F32),
        ],
        scratch_shapes=[
            pltpu.VMEM((tm, 2 * heads * d_key), BF16),
            pltpu.VMEM((tm, 2 * heads * d_key), BF16),
            pltpu.VMEM((2 * heads, n_keys, tm), F32),
            pltpu.VMEM((hk, tm), I32),
            pltpu.VMEM((hk, tm), F32),
        ],
        compiler_params=pltpu.CompilerParams(
            dimension_semantics=("parallel",), vmem_limit_bytes=VMEM_LIMIT_BYTES),
        name="route",
    )(h2, g.reshape(1, d), w_q, sk)


GATHER_ROWS = 16
GATHER_BUFFERS = 4
LANE_STEPS = 4
BLOCK_LANES = LANE_STEPS * SC_LANES
BLOCKS_PER_TILE = LANES // BLOCK_LANES
GROUP_TOKENS = 16


def _sc_mesh():
    return plsc.VectorSubcoreMesh(core_axis_name="c", subcore_axis_name="s")


def _sc_geometry(t, hk):
    per_worker = t // SC_WORKERS
    group = min(per_worker, GROUP_TOKENS)
    chunks = hk // GATHER_ROWS
    assert t % SC_WORKERS == 0 and per_worker % group == 0 and group % SUBLANES == 0
    assert hk % GATHER_ROWS == 0 and chunks % GATHER_BUFFERS == 0 and GATHER_ROWS == SC_LANES
    return per_worker, group, chunks


def _token_stream(first_token, per_worker, group, chunks, idx_hbm, side_hbm, table_hbm, out_hbm,
                  idx_v, side_v, o_v, ring, sems, stage_sems, compute):
    n_groups = per_worker // group

    def stage(grp, par):
        tok0 = pl.multiple_of(first_token + grp * group, SUBLANES)
        return (pltpu.make_async_copy(idx_hbm.at[pl.ds(tok0, group)], idx_v.at[par], stage_sems.at[par]),
                pltpu.make_async_copy(side_hbm.at[pl.ds(tok0, group)], side_v.at[par], stage_sems.at[par]))

    def gather(par, g, c, slot):
        rows = idx_v[par, g, pl.ds(c * GATHER_ROWS, GATHER_ROWS)]
        return pltpu.make_async_copy(table_hbm.at[rows], ring.at[slot], sems.at[slot])

    for cp in stage(0, 0):
        cp.start()
    for cp in stage(0, 0):
        cp.wait()
    for c in range(GATHER_BUFFERS):
        gather(0, 0, c, c).start()

    @pl.loop(0, per_worker)
    def _(tk):
        grp = tk // group
        g = tk % group
        par = grp % 2
        more_groups = grp + 1 < n_groups

        @pl.when((g == 0) & more_groups)
        def _():
            for cp in stage(grp + 1, 1 - par):
                cp.start()

        @pl.when((g == group - 1) & more_groups)
        def _():
            for cp in stage(grp + 1, 1 - par):
                cp.wait()

        for c in range(chunks):
            slot = c % GATHER_BUFFERS
            gather(par, g, c, slot).wait()
            compute(par, g, c, slot)
            nxt = c + GATHER_BUFFERS
            if nxt < chunks:
                gather(par, g, nxt, slot).start()
            else:
                @pl.when(tk + 1 < per_worker)
                def _():
                    gather(((tk + 1) // group) % 2, (tk + 1) % group, nxt - chunks, slot).start()

        @pl.when(g == group - 1)
        def _():
            tok0 = pl.multiple_of(first_token + grp * group, SUBLANES)
            pltpu.sync_copy(o_v, out_hbm.at[pl.ds(tok0, group)])


def _row_tiles(table):
    e, d = table.shape
    assert d % LANES == 0
    return table.reshape(e, d // LANES, LANES)


def _peer_dots(xn, idx, table, after):
    t, d = xn.shape
    hk = idx.shape[1]
    per_worker, group, chunks = _sc_geometry(t, hk)
    tiles = table.shape[1]

    def body(xn_hbm, idx_hbm, table_hbm, after_hbm, out_hbm, idx_v, x_v, o_v, ring, sems, stage_sems):
        del after_hbm
        wid = lax.axis_index("c") * SC_SUBCORES + lax.axis_index("s")
        lane = lax.iota(I32, SC_LANES)

        def compute(par, g, c, slot):
            def col_block(cb, accs):
                tile, lane0 = cb // BLOCKS_PER_TILE, (cb % BLOCKS_PER_TILE) * BLOCK_LANES
                for u in range(LANE_STEPS):
                    xc = x_v[par, g, pl.ds(cb * BLOCK_LANES + u * SC_LANES, SC_LANES)]
                    accs = tuple(a + ring[slot, r, tile, pl.ds(lane0 + u * SC_LANES, SC_LANES)] * xc
                                 for r, a in enumerate(accs))
                return accs

            zero = jnp.zeros((SC_LANES,), F32)
            accs = lax.fori_loop(0, tiles * BLOCKS_PER_TILE, col_block, (zero,) * GATHER_ROWS)
            tot = zero
            for r in range(GATHER_ROWS):
                tot = jnp.where(lane == r, jnp.sum(accs[r]), tot)
            o_v[g, pl.ds(c * GATHER_ROWS, GATHER_ROWS)] = tot

        _token_stream(wid * per_worker, per_worker, group, chunks, idx_hbm, xn_hbm, table_hbm, out_hbm,
                      idx_v, x_v, o_v, ring, sems, stage_sems, compute)

    return pl.kernel(
        body,
        out_type=jax.ShapeDtypeStruct((t, hk), F32),
        mesh=_sc_mesh(),
        scratch_types=[
            pltpu.VMEM((2, group, hk), I32),
            pltpu.VMEM((2, group, d), F32),
            pltpu.VMEM((group, hk), F32),
            pltpu.VMEM((GATHER_BUFFERS, GATHER_ROWS, tiles, LANES), F32),
            pltpu.SemaphoreType.DMA((GATHER_BUFFERS,)),
            pltpu.SemaphoreType.DMA((2,)),
        ],
        compiler_params=pltpu.CompilerParams(needs_layout_passes=False),
        name="peer_dots",
    )(xn, idx, table, after)


def _peer_mix(w, idx, table):
    t, hk = w.shape
    tiles = table.shape[1]
    d = tiles * LANES
    per_worker, group, chunks = _sc_geometry(t, hk)

    def body(w_hbm, idx_hbm, table_hbm, out_hbm, idx_v, w_v, o_v, ring, sems, stage_sems, acc_v):
        wid = lax.axis_index("c") * SC_SUBCORES + lax.axis_index("s")

        def compute(par, g, c, slot):
            wv = w_v[par, g, pl.ds(c * GATHER_ROWS, GATHER_ROWS)]
            ws = [jnp.full((SC_LANES,), wv[r], F32) for r in range(GATHER_ROWS)]

            @plsc.parallel_loop(0, tiles * BLOCKS_PER_TILE)
            def _(cb):
                tile, lane0 = cb // BLOCKS_PER_TILE, (cb % BLOCKS_PER_TILE) * BLOCK_LANES
                sums = []
                for u in range(LANE_STEPS):
                    part = [None] * 4
                    for r in range(GATHER_ROWS):
                        term = ws[r] * ring[slot, r, tile, pl.ds(lane0 + u * SC_LANES, SC_LANES)]
                        part[r % 4] = term if part[r % 4] is None else part[r % 4] + term
                    sums.append((part[0] + part[1]) + (part[2] + part[3]))
                for u in range(LANE_STEPS):
                    col = cb * BLOCK_LANES + u * SC_LANES
                    if c == 0:
                        acc_v[pl.ds(col, SC_LANES)] = sums[u]
                    elif c < chunks - 1:
                        acc_v[pl.ds(col, SC_LANES)] = acc_v[pl.ds(col, SC_LANES)] + sums[u]
                    else:
                        o_v[g, pl.ds(col, SC_LANES)] = acc_v[pl.ds(col, SC_LANES)] + sums[u]

        _token_stream(wid * per_worker, per_worker, group, chunks, idx_hbm, w_hbm, table_hbm, out_hbm,
                      idx_v, w_v, o_v, ring, sems, stage_sems, compute)

    return pl.kernel(
        body,
        out_type=jax.ShapeDtypeStruct((t, d), F32),
        mesh=_sc_mesh(),
        scratch_types=[
            pltpu.VMEM((2, group, hk), I32),
            pltpu.VMEM((2, group, hk), F32),
            pltpu.VMEM((group, d), F32),
            pltpu.VMEM((GATHER_BUFFERS, GATHER_ROWS, tiles, LANES), F32),
            pltpu.SemaphoreType.DMA((GATHER_BUFFERS,)),
            pltpu.SemaphoreType.DMA((2,)),
            pltpu.VMEM((d,), F32),
        ],
        compiler_params=pltpu.CompilerParams(needs_layout_passes=False),
        name="peer_mix",
    )(w, idx, table)


def _gate_body(d_ref, gate_ref, w_ref):
    x = d_ref[...]
    sqrt_half = np.sqrt(0.5).astype(np.float32)
    w_ref[...] = gate_ref[...] * (0.5 * x * (1.0 + lax.erf(x * sqrt_half)))


def _gate(dots, gate):
    t, hk = dots.shape
    tm = min(t, 1024)
    assert t % tm == 0
    spec = pl.BlockSpec((tm, hk), lambda i: (i, 0))
    return pl.pallas_call(
        _gate_body, grid=(t // tm,), in_specs=[spec, spec], out_specs=spec,
        out_shape=jax.ShapeDtypeStruct((t, hk), F32),
        compiler_params=pltpu.CompilerParams(dimension_semantics=("parallel",)),
        name="gate",
    )(dots, gate)


def _residual_body(y_all_ref, h_ref, p_ref, g_ref, y_ref, *, final):
    del y_all_ref
    x = h_ref[...] + p_ref[...]
    y_ref[...] = _rms(x, g_ref[...]) if final else x


def _residual(y_all, y0, h2, h0, p, g, final):
    t, d = p.shape
    tm = min(t, 1024)
    assert t % tm == 0 and y0 % tm == 0 and h0 % tm == 0
    y_first, h_first = y0 // tm, h0 // tm
    return pl.pallas_call(
        functools.partial(_residual_body, final=final),
        grid=(t // tm,),
        in_specs=[pl.BlockSpec(memory_space=pl.ANY),
                  pl.BlockSpec((tm, d), lambda i: (h_first + i, 0)),
                  pl.BlockSpec((tm, d), lambda i: (i, 0)),
                  _const_spec((1, d))],
        out_specs=pl.BlockSpec((tm, d), lambda i: (y_first + i, 0)),
        out_shape=jax.ShapeDtypeStruct(y_all.shape, F32),
        input_output_aliases={0: 0},
        compiler_params=pltpu.CompilerParams(dimension_semantics=("parallel",)),
        name="residual",
    )(y_all, h2, p, g.reshape(1, d))


SEQ_PARTS = 8


def _num_parts(n, l):
    per_part = (n // SEQ_PARTS) * l
    ok = n % SEQ_PARTS == 0 and per_part % (SC_WORKERS * SUBLANES) == 0
    return SEQ_PARTS if ok else 1


MIX_LAG = 2
FIRST_PIECE_TOKENS = 1024


def _token_pieces(q, tokens):
    quantum = SC_WORKERS * GROUP_TOKENS
    rest = tokens - FIRST_PIECE_TOKENS
    if q == 0 and rest > 0 and FIRST_PIECE_TOKENS % quantum == 0 and rest % quantum == 0:
        return [(0, FIRST_PIECE_TOKENS), (FIRST_PIECE_TOKENS, rest)]
    return [(0, tokens)]


def _trunk(x, hist_a, hist_b, norm_mix_g, w_in, conv_a_w, conv_b_w, conv_b_b, conv_norm_g, conv_norm_b,
           out_norm_a_g, out_norm_b_g, w_out, norm_ffn_g, w_q, sub_keys, u_tiles, v_tiles, final_norm_g):
    n, l, d = x.shape
    depth = len(w_in)
    parts = _num_parts(n, l)
    seqs = n // parts
    new_a, new_b = [], []
    for layer in range(depth):
        last = layer == depth - 1
        y_all = jnp.zeros((n * l, d), F32)
        nas, nbs, mixed = [], [], []
        for q in range(parts):
            h, na, nb = _mixer(x, q * seqs, seqs, hist_a[layer], hist_b[layer], norm_mix_g[layer], w_in[layer],
                               conv_a_w[layer], conv_b_w[layer], conv_b_b[layer], conv_norm_g[layer],
                               conv_norm_b[layer], out_norm_a_g[layer], out_norm_b_g[layer], w_out[layer])
            h2 = h.reshape(seqs * l, d)
            for t0, tn in _token_pieces(q, seqs * l):
                xn, idx, gate = _route(h2, t0, tn, norm_ffn_g[layer], w_q[layer], sub_keys[layer])
                after = mixed[-MIX_LAG] if len(mixed) >= MIX_LAG else idx
                dots = _peer_dots(xn, idx, u_tiles[layer], after)
                w = _gate(dots, gate)
                p = _peer_mix(w, idx, v_tiles[layer])
                mixed.append(p)
                y_all = _residual(y_all, q * seqs * l + t0, h2, t0, p,
                                  final_norm_g if last else jnp.ones((d,), F32), final=last)
            nas.append(na)
            nbs.append(nb)
        x = y_all.reshape(n, l, d)
        new_a.append(jnp.concatenate(nas, axis=0))
        new_b.append(jnp.concatenate(nbs, axis=0))
    return x, jnp.stack(new_a), jnp.stack(new_b)


def kernel(x_prompt, x_sample, cache_conv_a, cache_conv_b, norm_mix_g, w_in, conv_a_w, conv_b_w, conv_b_b,
           conv_norm_g, conv_norm_b, out_norm_a_g, out_norm_b_g, w_out, norm_ffn_g, w_q, sub_keys,
           u_experts, v_experts, final_norm_g):
    depth = w_in.shape[0]
    _, heads, _, n_keys, d_key = sub_keys.shape
    u_tiles = [_row_tiles(u_experts[layer]) for layer in range(depth)]
    v_tiles = [_row_tiles(v_experts[layer]) for layer in range(depth)]
    w_in_s = [_split(w_in[layer]) for layer in range(depth)]
    w_out_s = [_split(w_out[layer]) for layer in range(depth)]
    w_q_s = [_split(w_q[layer]) for layer in range(depth)]
    sk_s = [_split(sub_keys[layer].reshape(heads * 2 * n_keys, d_key)).reshape(2, heads * 2, n_keys, d_key)
            for layer in range(depth)]
    weights = (norm_mix_g, w_in_s, conv_a_w, conv_b_w, conv_b_b, conv_norm_g, conv_norm_b, out_norm_a_g,
               out_norm_b_g, w_out_s, norm_ffn_g, w_q_s, sk_s, u_tiles, v_tiles, final_norm_g)
    n = x_prompt.shape[0]
    y_sample, conv_a_sample, conv_b_sample = _trunk(x_sample, cache_conv_a, cache_conv_b, *weights)
    zeros_a = jnp.zeros((depth, n) + cache_conv_a.shape[2:], x_prompt.dtype)
    zeros_b = jnp.zeros((depth, n) + cache_conv_b.shape[2:], x_prompt.dtype)
    y_prompt, conv_a_prompt, conv_b_prompt = _trunk(x_prompt, zeros_a, zeros_b, *weights)
    return (y_prompt, y_sample, conv_a_prompt, conv_b_prompt, conv_a_sample, conv_b_sample)
```

```python
import functools

import numpy as np
import jax
import jax.numpy as jnp
from jax import lax
from jax.experimental import pallas as pl
from jax.experimental.pallas import tpu as pltpu
from jax.experimental.pallas import tpu_sc as plsc

F32 = jnp.float32
I32 = jnp.int32
BF16 = jnp.bfloat16
EPS = 1e-6

LANES = 128
SUBLANES = 8
VMEM_LIMIT_BYTES = 56 * 1024 * 1024
SC_CORES = 2
SC_SUBCORES = 16
SC_LANES = 16
SC_WORKERS = SC_CORES * SC_SUBCORES

PEER_TOPK = 16


def _rms(x, g):
    return x * lax.rsqrt(jnp.mean(x * x, axis=-1, keepdims=True) + EPS) * g


def _const_spec(shape):
    zeros = (0,) * len(shape)
    return pl.BlockSpec(shape, lambda *_: zeros, pipeline_mode=pl.Buffered(1))


def _hi_lo(x):
    hi = x.astype(BF16)
    return hi, (x - hi.astype(F32)).astype(BF16)


def _dot3(a, b_hi, b_lo, dims=(((1,), (0,)), ((), ()))):
    a_hi, a_lo = _hi_lo(a)
    dot = functools.partial(lax.dot_general, dimension_numbers=dims, preferred_element_type=F32)
    return (dot(a_hi, b_lo) + dot(a_lo, b_hi)) + dot(a_hi, b_hi)


def _split_body(w_ref, s_ref):
    hi, lo = _hi_lo(w_ref[...])
    s_ref[0] = hi
    s_ref[1] = lo


def _split(w):
    r, c = w.shape
    tm = min(r, 256)
    assert r % tm == 0
    return pl.pallas_call(
        _split_body, grid=(r // tm,),
        in_specs=[pl.BlockSpec((tm, c), lambda i: (i, 0))],
        out_specs=pl.BlockSpec((2, tm, c), lambda i: (0, i, 0)),
        out_shape=jax.ShapeDtypeStruct((2, r, c), BF16),
        compiler_params=pltpu.CompilerParams(dimension_semantics=("parallel",)),
        name="split",
    )(w)


CONV_ROW_BLOCK = 64


def _mixer_body(x_ref, ha_ref, hb_ref, gmix_ref, win_ref, caw_ref, cbw_ref, cbb_ref, cng_ref, cnb_ref,
                ga_ref, gb_ref, wout_ref, h_ref, na_ref, nb_ref, xa_s, xb_s, cb_s,
                *, tm, wa, wb, ka, kb, pad_a, pad_b):
    i = pl.program_id(1)

    @pl.when(i == 0)
    def _():
        xa_s[0:pad_a] = ha_ref[...]
        xb_s[0:pad_b] = hb_ref[...]

    x = x_ref[...]
    xn = _rms(x, gmix_ref[...])
    proj = _dot3(xn, win_ref[0], win_ref[1])
    h_a = proj[:, 0:wa]
    c_a = proj[:, wa:2 * wa]
    b_a = proj[:, 2 * wa:3 * wa]
    v_b = proj[:, 3 * wa:3 * wa + wb]
    g_b = proj[:, 3 * wa + wb:3 * wa + 2 * wb]

    xa_s[pad_a:pad_a + tm] = c_a * h_a
    conv_a = caw_ref[0:1, :] * xa_s[pad_a - (ka - 1):pad_a - (ka - 1) + tm]
    for k in range(1, ka):
        lo = pad_a - (ka - 1) + k
        conv_a = conv_a + caw_ref[k:k + 1, :] * xa_s[lo:lo + tm]
    y_a = _rms(b_a * conv_a, ga_ref[...])

    xb_s[pad_b:pad_b + tm] = v_b * jax.nn.sigmoid(g_b)
    rb = min(tm, CONV_ROW_BLOCK)
    for r0 in range(0, tm, rb):
        base = r0 + pad_b - (kb - 1)
        acc = cbw_ref[0:1, :] * xb_s[base:base + rb]
        for k in range(1, kb):
            acc = acc + cbw_ref[k:k + 1, :] * xb_s[base + k:base + k + rb]
        cb_s[r0:r0 + rb] = acc + cbb_ref[...]
    cb = cb_s[...]
    mu = jnp.mean(cb, axis=-1, keepdims=True)
    xc = cb - mu
    ln = xc * lax.rsqrt(jnp.mean(xc * xc, axis=-1, keepdims=True) + EPS) * cng_ref[...] + cnb_ref[...]
    z = ln * jax.nn.sigmoid(ln)
    y_b = _rms(z, gb_ref[...])

    y = (_dot3(y_a, wout_ref[0, 0:wa, :], wout_ref[1, 0:wa, :])
         + _dot3(y_b, wout_ref[0, wa:wa + wb, :], wout_ref[1, wa:wa + wb, :]))
    h_ref[...] = x + y

    xa_s[0:pad_a] = xa_s[tm:tm + pad_a]
    xb_s[0:pad_b] = xb_s[tm:tm + pad_b]

    @pl.when(i == pl.num_programs(1) - 1)
    def _():
        na_ref[...] = xa_s[0:pad_a]
        nb_ref[...] = xb_s[0:pad_b]


def _mixer(x, seq0, n, hist_a, hist_b, gmix, w_in, caw, cbw, cbb, cng, cnb, ga, gb, w_out):
    _, l, d = x.shape
    ka, wa = caw.shape
    kb, wb = cbw.shape
    pad_a = SUBLANES
    pad_b = -(-(kb - 1) // SUBLANES) * SUBLANES
    tm = min(l, 512)
    assert l % tm == 0 and tm % min(tm, CONV_ROW_BLOCK) == 0
    assert tm >= pad_b and tm % SUBLANES == 0 and ka - 1 <= pad_a
    ha = jnp.pad(hist_a, ((0, 0), (pad_a - (ka - 1), 0), (0, 0)))
    hb = jnp.pad(hist_b, ((0, 0), (pad_b - (kb - 1), 0), (0, 0)))
    row = lambda v: v.reshape(1, -1)
    body = functools.partial(_mixer_body, tm=tm, wa=wa, wb=wb, ka=ka, kb=kb, pad_a=pad_a, pad_b=pad_b)
    h, na, nb = pl.pallas_call(
        body,
        grid=(n, l // tm),
        in_specs=[
            pl.BlockSpec((None, tm, d), lambda b, i: (seq0 + b, i, 0)),
            pl.BlockSpec((None, pad_a, wa), lambda b, i: (seq0 + b, 0, 0)),
            pl.BlockSpec((None, pad_b, wb), lambda b, i: (seq0 + b, 0, 0)),
            _const_spec((1, d)),
            _const_spec(w_in.shape),
            _const_spec(caw.shape),
            _const_spec(cbw.shape),
            _const_spec((1, wb)),
            _const_spec((1, wb)),
            _const_spec((1, wb)),
            _const_spec((1, wa)),
            _const_spec((1, wb)),
            _const_spec(w_out.shape),
        ],
        out_specs=[
            pl.BlockSpec((None, tm, d), lambda b, i: (b, i, 0)),
            pl.BlockSpec((None, pad_a, wa), lambda b, i: (b, 0, 0)),
            pl.BlockSpec((None, pad_b, wb), lambda b, i: (b, 0, 0)),
        ],
        out_shape=[
            jax.ShapeDtypeStruct((n, l, d), F32),
            jax.ShapeDtypeStruct((n, pad_a, wa), F32),
            jax.ShapeDtypeStruct((n, pad_b, wb), F32),
        ],
        scratch_shapes=[
            pltpu.VMEM((pad_a + tm, wa), F32),
            pltpu.VMEM((pad_b + tm, wb), F32),
            pltpu.VMEM((tm, wb), F32),
        ],
        compiler_params=pltpu.CompilerParams(
            dimension_semantics=("parallel", "arbitrary"), vmem_limit_bytes=VMEM_LIMIT_BYTES),
        name="mixer",
    )(x, ha, hb, row(gmix), w_in, caw, cbw, row(cbb), row(cng), row(cnb), row(ga), row(gb), w_out)
    return h, na[:, pad_a - (ka - 1):], nb[:, pad_b - (kb - 1):]


def _top_rows(s, k):
    n, lanes = s.shape
    rows = lax.broadcasted_iota(I32, s.shape, 0)
    slot = lax.broadcasted_iota(I32, (k, lanes), 0)
    vals = jnp.zeros((k, lanes), F32)
    ids = jnp.zeros((k, lanes), I32)
    for it in range(k):
        m = jnp.max(s, axis=0, keepdims=True)
        am = jnp.min(jnp.where(s == m, rows, n), axis=0, keepdims=True)
        vals = jnp.where(slot == it, m, vals)
        ids = jnp.where(slot == it, am, ids)
        s = jnp.where(rows == am, -jnp.inf, s)
    return vals, ids


def _combine(v1, i1, v2, i2, n_keys):
    k, lanes = v1.shape
    half = k // 2
    vals, pos, eid = [], [], []

    def add(a_lo, a_n, b_lo, b_n):
        vals.append(v1[a_lo:a_lo + a_n] + v2[b_lo:b_lo + b_n])
        eid.append(i1[a_lo:a_lo + a_n] * n_keys + i2[b_lo:b_lo + b_n])
        r = lax.broadcasted_iota(I32, (max(a_n, b_n), lanes), 0)
        pos.append((a_lo + r) * k + b_lo if a_n > 1 else a_lo * k + b_lo + r)

    add(0, 1, 0, k)
    for a in range(1, half):
        add(a, 1, 0, half)
    add(half, k - half, 0, 1)
    cand = jnp.concatenate(vals, axis=0)
    pos = jnp.concatenate(pos, axis=0)
    eid = jnp.concatenate(eid, axis=0)

    slot = lax.broadcasted_iota(I32, (k, lanes), 0)
    top_s = jnp.zeros((k, lanes), F32)
    top_e = jnp.zeros((k, lanes), I32)
    for it in range(k):
        m = jnp.max(cand, axis=0, keepdims=True)
        sel = jnp.min(jnp.where(cand == m, pos, k * k), axis=0, keepdims=True)
        hit = pos == sel
        e = jnp.max(jnp.where(hit, eid, -1), axis=0, keepdims=True)
        top_s = jnp.where(slot == it, m, top_s)
        top_e = jnp.where(slot == it, e, top_e)
        cand = jnp.where(hit, -jnp.inf, cand)
    return top_s, top_e


def _route_body(h_ref, g_ref, wq_ref, sk_ref, xn_ref, idx_ref, gate_ref, qh_s, ql_s, st_s, et_s, gt_s,
                *, tm, heads, n_keys, d_key):
    hn = _rms(h_ref[...], g_ref[...])
    xn_ref[...] = hn
    qh_s[...], ql_s[...] = _hi_lo(_dot3(hn, wq_ref[0], wq_ref[1]))
    dot_nt = functools.partial(lax.dot_general, dimension_numbers=(((1,), (1,)), ((), ())),
                               preferred_element_type=F32)
    for hp in range(2 * heads):
        cols = slice(hp * d_key, (hp + 1) * d_key)
        st_s[hp] = ((dot_nt(sk_ref[0, hp], ql_s[:, cols]) + dot_nt(sk_ref[1, hp], qh_s[:, cols]))
                    + dot_nt(sk_ref[0, hp], qh_s[:, cols]))
    n_chunk = tm // LANES

    def per_head_chunk(j, carry):
        hd = j // n_chunk
        col = pl.multiple_of((j % n_chunk) * LANES, LANES)
        v1, i1 = _top_rows(st_s[2 * hd, :, pl.ds(col, LANES)], PEER_TOPK)
        v2, i2 = _top_rows(st_s[2 * hd + 1, :, pl.ds(col, LANES)], PEER_TOPK)
        top_s, top_e = _combine(v1, i1, v2, i2, n_keys)
        p = jnp.exp(top_s - top_s[0:1])
        gate = p / jnp.sum(p, axis=0, keepdims=True)
        row = pl.multiple_of(hd * PEER_TOPK, PEER_TOPK)
        et_s[pl.ds(row, PEER_TOPK), pl.ds(col, LANES)] = top_e
        gt_s[pl.ds(row, PEER_TOPK), pl.ds(col, LANES)] = gate
        return carry

    lax.fori_loop(0, heads * n_chunk, per_head_chunk, 0)
    idx_ref[...] = et_s[...].T
    gate_ref[...] = gt_s[...].T


def _route(h2, tok0, t, g, w_q, sk):
    d = h2.shape[1]
    _, heads2, n_keys, d_key = sk.shape
    heads = heads2 // 2
    hk = heads * PEER_TOPK
    tm = min(t, 512)
    assert t % tm == 0 and tok0 % tm == 0 and tm % LANES == 0 and d_key == LANES and n_keys % SUBLANES == 0
    first = tok0 // tm
    body = functools.partial(_route_body, tm=tm, heads=heads, n_keys=n_keys, d_key=d_key)
    return pl.pallas_call(
        body,
        grid=(t // tm,),
        in_specs=[
            pl.BlockSpec((tm, d), lambda i: (first + i, 0)),
            _const_spec((1, d)),
            _const_spec(w_q.shape),
            _const_spec(sk.shape),
        ],
        out_specs=[
            pl.BlockSpec((tm, d), lambda i: (i, 0)),
            pl.BlockSpec((tm, hk), lambda i: (i, 0)),
            pl.BlockSpec((tm, hk), lambda i: (i, 0)),
        ],
        out_shape=[
            jax.ShapeDtypeStruct((t, d), F32),
            jax.ShapeDtypeStruct((t, hk), I32),
            jax.ShapeDtypeStruct((t, hk), F32),
        ],
        scratch_shapes=[
            pltpu.VMEM((tm, 2 * heads * d_key), BF16),
            pltpu.VMEM((tm, 2 * heads * d_key), BF16),
            pltpu.VMEM((2 * heads, n_keys, tm), F32),
            pltpu.VMEM((hk, tm), I32),
            pltpu.VMEM((hk, tm), F32),
        ],
        compiler_params=pltpu.CompilerParams(
            dimension_semantics=("parallel",), vmem_limit_bytes=VMEM_LIMIT_BYTES),
        name="route",
    )(h2, g.reshape(1, d), w_q, sk)


GATHER_ROWS = 16
GATHER_BUFFERS = 4
LANE_STEPS = 4
BLOCK_LANES = LANE_STEPS * SC_LANES
BLOCKS_PER_TILE = LANES // BLOCK_LANES
MIX_LANE_STEPS = 2
MIX_BLOCK_LANES = MIX_LANE_STEPS * SC_LANES
MIX_BLOCKS_PER_TILE = LANES // MIX_BLOCK_LANES
GROUP_TOKENS = 16


def _sc_mesh():
    return plsc.VectorSubcoreMesh(core_axis_name="c", subcore_axis_name="s")


def _sc_geometry(t, hk):
    per_worker = t // SC_WORKERS
    group = min(per_worker, GROUP_TOKENS)
    chunks = hk // GATHER_ROWS
    assert t % SC_WORKERS == 0 and per_worker % group == 0 and group % SUBLANES == 0
    assert hk % GATHER_ROWS == 0 and chunks % GATHER_BUFFERS == 0 and GATHER_ROWS == SC_LANES
    return per_worker, group, chunks


def _token_stream(first_token, per_worker, group, chunks, idx_hbm, side_hbm, table_hbm, out_hbm,
                  idx_v, side_v, o_v, ring, sems, stage_sems, compute):
    n_groups = per_worker // group

    def stage(grp, par):
        tok0 = pl.multiple_of(first_token + grp * group, SUBLANES)
        return (pltpu.make_async_copy(idx_hbm.at[pl.ds(tok0, group)], idx_v.at[par], stage_sems.at[par]),
                pltpu.make_async_copy(side_hbm.at[pl.ds(tok0, group)], side_v.at[par], stage_sems.at[par]))

    def gather(par, g, c, slot):
        rows = idx_v[par, g, pl.ds(c * GATHER_ROWS, GATHER_ROWS)]
        return pltpu.make_async_copy(table_hbm.at[rows], ring.at[slot], sems.at[slot])

    for cp in stage(0, 0):
        cp.start()
    for cp in stage(0, 0):
        cp.wait()
    for c in range(GATHER_BUFFERS):
        gather(0, 0, c, c).start()

    @pl.loop(0, per_worker)
    def _(tk):
        grp = tk // group
        g = tk % group
        par = grp % 2
        more_groups = grp + 1 < n_groups

        @pl.when((g == 0) & more_groups)
        def _():
            for cp in stage(grp + 1, 1 - par):
                cp.start()

        @pl.when((g == group - 1) & more_groups)
        def _():
            for cp in stage(grp + 1, 1 - par):
                cp.wait()

        for c in range(chunks):
            slot = c % GATHER_BUFFERS
            gather(par, g, c, slot).wait()
            compute(par, g, c, slot)
            nxt = c + GATHER_BUFFERS
            if nxt < chunks:
                gather(par, g, nxt, slot).start()
            else:
                @pl.when(tk + 1 < per_worker)
                def _():
                    gather(((tk + 1) // group) % 2, (tk + 1) % group, nxt - chunks, slot).start()

        @pl.when(g == group - 1)
        def _():
            tok0 = pl.multiple_of(first_token + grp * group, SUBLANES)
            pltpu.sync_copy(o_v, out_hbm.at[pl.ds(tok0, group)])


def _row_tiles(table):
    e, d = table.shape
    assert d % LANES == 0
    return table.reshape(e, d // LANES, LANES)


def _peer_dots(xn, idx, table, after):
    t, d = xn.shape
    hk = idx.shape[1]
    per_worker, group, chunks = _sc_geometry(t, hk)
    tiles = table.shape[1]

    def body(xn_hbm, idx_hbm, table_hbm, after_hbm, out_hbm, idx_v, x_v, o_v, ring, sems, stage_sems):
        del after_hbm
        wid = lax.axis_index("c") * SC_SUBCORES + lax.axis_index("s")
        lane = lax.iota(I32, SC_LANES)

        def compute(par, g, c, slot):
            def col_block(cb, accs):
                tile, lane0 = cb // BLOCKS_PER_TILE, (cb % BLOCKS_PER_TILE) * BLOCK_LANES
                for u in range(LANE_STEPS):
                    xc = x_v[par, g, pl.ds(cb * BLOCK_LANES + u * SC_LANES, SC_LANES)]
                    accs = tuple(a + ring[slot, r, tile, pl.ds(lane0 + u * SC_LANES, SC_LANES)] * xc
                                 for r, a in enumerate(accs))
                return accs

            zero = jnp.zeros((SC_LANES,), F32)
            accs = lax.fori_loop(0, tiles * BLOCKS_PER_TILE, col_block, (zero,) * GATHER_ROWS)
            tot = zero
            for r in range(GATHER_ROWS):
                tot = jnp.where(lane == r, jnp.sum(accs[r]), tot)
            o_v[g, pl.ds(c * GATHER_ROWS, GATHER_ROWS)] = tot

        _token_stream(wid * per_worker, per_worker, group, chunks, idx_hbm, xn_hbm, table_hbm, out_hbm,
                      idx_v, x_v, o_v, ring, sems, stage_sems, compute)

    return pl.kernel(
        body,
        out_type=jax.ShapeDtypeStruct((t, hk), F32),
        mesh=_sc_mesh(),
        scratch_types=[
            pltpu.VMEM((2, group, hk), I32),
            pltpu.VMEM((2, group, d), F32),
            pltpu.VMEM((group, hk), F32),
            pltpu.VMEM((GATHER_BUFFERS, GATHER_ROWS, tiles, LANES), F32),
            pltpu.SemaphoreType.DMA((GATHER_BUFFERS,)),
            pltpu.SemaphoreType.DMA((2,)),
        ],
        compiler_params=pltpu.CompilerParams(needs_layout_passes=False),
        name="peer_dots",
    )(xn, idx, table, after)


def _peer_mix(w, idx, table):
    t, hk = w.shape
    tiles = table.shape[1]
    d = tiles * LANES
    per_worker, group, chunks = _sc_geometry(t, hk)

    def body(w_hbm, idx_hbm, table_hbm, out_hbm, idx_v, w_v, o_v, ring, sems, stage_sems, acc_v):
        wid = lax.axis_index("c") * SC_SUBCORES + lax.axis_index("s")

        def compute(par, g, c, slot):
            wv = w_v[par, g, pl.ds(c * GATHER_ROWS, GATHER_ROWS)]
            ws = [jnp.full((SC_LANES,), wv[r], F32) for r in range(GATHER_ROWS)]

            @plsc.parallel_loop(0, tiles * MIX_BLOCKS_PER_TILE)
            def _(cb):
                tile, lane0 = cb // MIX_BLOCKS_PER_TILE, (cb % MIX_BLOCKS_PER_TILE) * MIX_BLOCK_LANES
                sums = []
                for u in range(MIX_LANE_STEPS):
                    part = [None] * 4
                    for r in range(GATHER_ROWS):
                        term = ws[r] * ring[slot, r, tile, pl.ds(lane0 + u * SC_LANES, SC_LANES)]
                        part[r % 4] = term if part[r % 4] is None else part[r % 4] + term
                    sums.append((part[0] + part[1]) + (part[2] + part[3]))
                for u in range(MIX_LANE_STEPS):
                    col = cb * MIX_BLOCK_LANES + u * SC_LANES
                    if c == 0:
                        acc_v[pl.ds(col, SC_LANES)] = sums[u]
                    elif c < chunks - 1:
                        acc_v[pl.ds(col, SC_LANES)] = acc_v[pl.ds(col, SC_LANES)] + sums[u]
                    else:
                        o_v[g, pl.ds(col, SC_LANES)] = acc_v[pl.ds(col, SC_LANES)] + sums[u]

        _token_stream(wid * per_worker, per_worker, group, chunks, idx_hbm, w_hbm, table_hbm, out_hbm,
                      idx_v, w_v, o_v, ring, sems, stage_sems, compute)

    return pl.kernel(
        body,
        out_type=jax.ShapeDtypeStruct((t, d), F32),
        mesh=_sc_mesh(),
        scratch_types=[
            pltpu.VMEM((2, group, hk), I32),
            pltpu.VMEM((2, group, hk), F32),
            pltpu.VMEM((group, d), F32),
            pltpu.VMEM((GATHER_BUFFERS, GATHER_ROWS, tiles, LANES), F32),
            pltpu.SemaphoreType.DMA((GATHER_BUFFERS,)),
            pltpu.SemaphoreType.DMA((2,)),
            pltpu.VMEM((d,), F32),
        ],
        compiler_params=pltpu.CompilerParams(needs_layout_passes=False),
        name="peer_mix",
    )(w, idx, table)


def _gate_body(d_ref, gate_ref, w_ref):
    x = d_ref[...]
    sqrt_half = np.sqrt(0.5).astype(np.float32)
    w_ref[...] = gate_ref[...] * (0.5 * x * (1.0 + lax.erf(x * sqrt_half)))


def _gate(dots, gate):
    t, hk = dots.shape
    tm = min(t, 1024)
    assert t % tm == 0
    spec = pl.BlockSpec((tm, hk), lambda i: (i, 0))
    return pl.pallas_call(
        _gate_body, grid=(t // tm,), in_specs=[spec, spec], out_specs=spec,
        out_shape=jax.ShapeDtypeStruct((t, hk), F32),
        compiler_params=pltpu.CompilerParams(dimension_semantics=("parallel",)),
        name="gate",
    )(dots, gate)


def _residual_body(y_all_ref, h_ref, p_ref, g_ref, y_ref, *, final):
    del y_all_ref
    x = h_ref[...] + p_ref[...]
    y_ref[...] = _rms(x, g_ref[...]) if final else x


def _residual(y_all, y0, h2, h0, p, g, final):
    t, d = p.shape
    tm = min(t, 1024)
    assert t % tm == 0 and y0 % tm == 0 and h0 % tm == 0
    y_first, h_first = y0 // tm, h0 // tm
    return pl.pallas_call(
        functools.partial(_residual_body, final=final),
        grid=(t // tm,),
        in_specs=[pl.BlockSpec(memory_space=pl.ANY),
                  pl.BlockSpec((tm, d), lambda i: (h_first + i, 0)),
                  pl.BlockSpec((tm, d), lambda i: (i, 0)),
                  _const_spec((1, d))],
        out_specs=pl.BlockSpec((tm, d), lambda i: (y_first + i, 0)),
        out_shape=jax.ShapeDtypeStruct(y_all.shape, F32),
        input_output_aliases={0: 0},
        compiler_params=pltpu.CompilerParams(dimension_semantics=("parallel",)),
        name="residual",
    )(y_all, h2, p, g.reshape(1, d))


SEQ_PARTS = 8


def _num_parts(n, l):
    per_part = (n // SEQ_PARTS) * l
    ok = n % SEQ_PARTS == 0 and per_part % (SC_WORKERS * SUBLANES) == 0
    return SEQ_PARTS if ok else 1


MIX_LAG = 2
FIRST_PIECE_TOKENS = 1024


def _token_pieces(q, tokens):
    quantum = SC_WORKERS * GROUP_TOKENS
    rest = tokens - FIRST_PIECE_TOKENS
    if q == 0 and rest > 0 and FIRST_PIECE_TOKENS % quantum == 0 and rest % quantum == 0:
        return [(0, FIRST_PIECE_TOKENS), (FIRST_PIECE_TOKENS, rest)]
    return [(0, tokens)]


def _trunk(x, hist_a, hist_b, norm_mix_g, w_in, conv_a_w, conv_b_w, conv_b_b, conv_norm_g, conv_norm_b,
           out_norm_a_g, out_norm_b_g, w_out, norm_ffn_g, w_q, sub_keys, u_tiles, v_tiles, final_norm_g):
    n, l, d = x.shape
    depth = len(w_in)
    parts = _num_parts(n, l)
    seqs = n // parts
    new_a, new_b = [], []
    for layer in range(depth):
        last = layer == depth - 1
        y_all = jnp.zeros((n * l, d), F32)
        nas, nbs, mixed = [], [], []
        for q in range(parts):
            h, na, nb = _mixer(x, q * seqs, seqs, hist_a[layer], hist_b[layer], norm_mix_g[layer], w_in[layer],
                               conv_a_w[layer], conv_b_w[layer], conv_b_b[layer], conv_norm_g[layer],
                               conv_norm_b[layer], out_norm_a_g[layer], out_norm_b_g[layer], w_out[layer])
            h2 = h.reshape(seqs * l, d)
            for t0, tn in _token_pieces(q, seqs * l):
                xn, idx, gate = _route(h2, t0, tn, norm_ffn_g[layer], w_q[layer], sub_keys[layer])
                after = mixed[-MIX_LAG] if len(mixed) >= MIX_LAG else idx
                dots = _peer_dots(xn, idx, u_tiles[layer], after)
                w = _gate(dots, gate)
                p = _peer_mix(w, idx, v_tiles[layer])
                mixed.append(p)
                y_all = _residual(y_all, q * seqs * l + t0, h2, t0, p,
                                  final_norm_g if last else jnp.ones((d,), F32), final=last)
            nas.append(na)
            nbs.append(nb)
        x = y_all.reshape(n, l, d)
        new_a.append(jnp.concatenate(nas, axis=0))
        new_b.append(jnp.concatenate(nbs, axis=0))
    return x, jnp.stack(new_a), jnp.stack(new_b)


def kernel(x_prompt, x_sample, cache_conv_a, cache_conv_b, norm_mix_g, w_in, conv_a_w, conv_b_w, conv_b_b,
           conv_norm_g, conv_norm_b, out_norm_a_g, out_norm_b_g, w_out, norm_ffn_g, w_q, sub_keys,
           u_experts, v_experts, final_norm_g):
    depth = w_in.shape[0]
    _, heads, _, n_keys, d_key = sub_keys.shape
    u_tiles = [_row_tiles(u_experts[layer]) for layer in range(depth)]
    v_tiles = [_row_tiles(v_experts[layer]) for layer in range(depth)]
    w_in_s = [_split(w_in[layer]) for layer in range(depth)]
    w_out_s = [_split(w_out[layer]) for layer in range(depth)]
    w_q_s = [_split(w_q[layer]) for layer in range(depth)]
    sk_s = [_split(sub_keys[layer].reshape(heads * 2 * n_keys, d_key)).reshape(2, heads * 2, n_keys, d_key)
            for layer in range(depth)]
    weights = (norm_mix_g, w_in_s, conv_a_w, conv_b_w, conv_b_b, conv_norm_g, conv_norm_b, out_norm_a_g,
               out_norm_b_g, w_out_s, norm_ffn_g, w_q_s, sk_s, u_tiles, v_tiles, final_norm_g)
    n = x_prompt.shape[0]
    y_sample, conv_a_sample, conv_b_sample = _trunk(x_sample, cache_conv_a, cache_conv_b, *weights)
    zeros_a = jnp.zeros((depth, n) + cache_conv_a.shape[2:], x_prompt.dtype)
    zeros_b = jnp.zeros((depth, n) + cache_conv_b.shape[2:], x_prompt.dtype)
    y_prompt, conv_a_prompt, conv_b_prompt = _trunk(x_prompt, zeros_a, zeros_b, *weights)
    return (y_prompt, y_sample, conv_a_prompt, conv_b_prompt, conv_a_sample, conv_b_sample)
```

```python
import functools

import numpy as np
import jax
import jax.numpy as jnp
from jax import lax
from jax.experimental import pallas as pl
from jax.experimental.pallas import tpu as pltpu
from jax.experimental.pallas import tpu_sc as plsc

F32 = jnp.float32
I32 = jnp.int32
BF16 = jnp.bfloat16
EPS = 1e-6

LANES = 128
SUBLANES = 8
VMEM_LIMIT_BYTES = 56 * 1024 * 1024
SC_CORES = 2
SC_SUBCORES = 16
SC_LANES = 16
SC_WORKERS = SC_CORES * SC_SUBCORES

PEER_TOPK = 16


def _rms(x, g):
    return x * lax.rsqrt(jnp.mean(x * x, axis=-1, keepdims=True) + EPS) * g


def _const_spec(shape):
    zeros = (0,) * len(shape)
    return pl.BlockSpec(shape, lambda *_: zeros, pipeline_mode=pl.Buffered(1))


def _hi_lo(x):
    hi = x.astype(BF16)
    return hi, (x - hi.astype(F32)).astype(BF16)


def _dot3(a, b_hi, b_lo, dims=(((1,), (0,)), ((), ()))):
    a_hi, a_lo = _hi_lo(a)
    dot = functools.partial(lax.dot_general, dimension_numbers=dims, preferred_element_type=F32)
    return (dot(a_hi, b_lo) + dot(a_lo, b_hi)) + dot(a_hi, b_hi)


def _split_body(w_ref, s_ref):
    hi, lo = _hi_lo(w_ref[...])
    s_ref[0] = hi
    s_ref[1] = lo


def _split(w):
    r, c = w.shape
    tm = min(r, 256)
    assert r % tm == 0
    return pl.pallas_call(
        _split_body, grid=(r // tm,),
        in_specs=[pl.BlockSpec((tm, c), lambda i: (i, 0))],
        out_specs=pl.BlockSpec((2, tm, c), lambda i: (0, i, 0)),
        out_shape=jax.ShapeDtypeStruct((2, r, c), BF16),
        compiler_params=pltpu.CompilerParams(dimension_semantics=("parallel",)),
        name="split",
    )(w)


CONV_ROW_BLOCK = 64


def _mixer_body(x_ref, ha_ref, hb_ref, gmix_ref, win_ref, caw_ref, cbw_ref, cbb_ref, cng_ref, cnb_ref,
                ga_ref, gb_ref, wout_ref, h_ref, na_ref, nb_ref, xa_s, xb_s, cb_s,
                *, tm, wa, wb, ka, kb, pad_a, pad_b):
    i = pl.program_id(1)

    @pl.when(i == 0)
    def _():
        xa_s[0:pad_a] = ha_ref[...]
        xb_s[0:pad_b] = hb_ref[...]

    x = x_ref[...]
    xn = _rms(x, gmix_ref[...])
    proj = _dot3(xn, win_ref[0], win_ref[1])
    h_a = proj[:, 0:wa]
    c_a = proj[:, wa:2 * wa]
    b_a = proj[:, 2 * wa:3 * wa]
    v_b = proj[:, 3 * wa:3 * wa + wb]
    g_b = proj[:, 3 * wa + wb:3 * wa + 2 * wb]

    xa_s[pad_a:pad_a + tm] = c_a * h_a
    conv_a = caw_ref[0:1, :] * xa_s[pad_a - (ka - 1):pad_a - (ka - 1) + tm]
    for k in range(1, ka):
        lo = pad_a - (ka - 1) + k
        conv_a = conv_a + caw_ref[k:k + 1, :] * xa_s[lo:lo + tm]
    y_a = _rms(b_a * conv_a, ga_ref[...])

    xb_s[pad_b:pad_b + tm] = v_b * jax.nn.sigmoid(g_b)
    rb = min(tm, CONV_ROW_BLOCK)
    for r0 in range(0, tm, rb):
        base = r0 + pad_b - (kb - 1)
        acc = cbw_ref[0:1, :] * xb_s[base:base + rb]
        for k in range(1, kb):
            acc = acc + cbw_ref[k:k + 1, :] * xb_s[base + k:base + k + rb]
        cb_s[r0:r0 + rb] = acc + cbb_ref[...]
    cb = cb_s[...]
    mu = jnp.mean(cb, axis=-1, keepdims=True)
    xc = cb - mu
    ln = xc * lax.rsqrt(jnp.mean(xc * xc, axis=-1, keepdims=True) + EPS) * cng_ref[...] + cnb_ref[...]
    z = ln * jax.nn.sigmoid(ln)
    y_b = _rms(z, gb_ref[...])

    y = (_dot3(y_a, wout_ref[0, 0:wa, :], wout_ref[1, 0:wa, :])
         + _dot3(y_b, wout_ref[0, wa:wa + wb, :], wout_ref[1, wa:wa + wb, :]))
    h_ref[...] = x + y

    xa_s[0:pad_a] = xa_s[tm:tm + pad_a]
    xb_s[0:pad_b] = xb_s[tm:tm + pad_b]

    @pl.when(i == pl.num_programs(1) - 1)
    def _():
        na_ref[...] = xa_s[0:pad_a]
        nb_ref[...] = xb_s[0:pad_b]


def _mixer(x, seq0, n, hist_a, hist_b, gmix, w_in, caw, cbw, cbb, cng, cnb, ga, gb, w_out):
    _, l, d = x.shape
    ka, wa = caw.shape
    kb, wb = cbw.shape
    pad_a = SUBLANES
    pad_b = -(-(kb - 1) // SUBLANES) * SUBLANES
    tm = min(l, 512)
    assert l % tm == 0 and tm % min(tm, CONV_ROW_BLOCK) == 0
    assert tm >= pad_b and tm % SUBLANES == 0 and ka - 1 <= pad_a
    ha = jnp.pad(hist_a, ((0, 0), (pad_a - (ka - 1), 0), (0, 0)))
    hb = jnp.pad(hist_b, ((0, 0), (pad_b - (kb - 1), 0), (0, 0)))
    row = lambda v: v.reshape(1, -1)
    body = functools.partial(_mixer_body, tm=tm, wa=wa, wb=wb, ka=ka, kb=kb, pad_a=pad_a, pad_b=pad_b)
    h, na, nb = pl.pallas_call(
        body,
        grid=(n, l // tm),
        in_specs=[
            pl.BlockSpec((None, tm, d), lambda b, i: (seq0 + b, i, 0)),
            pl.BlockSpec((None, pad_a, wa), lambda b, i: (seq0 + b, 0, 0)),
            pl.BlockSpec((None, pad_b, wb), lambda b, i: (seq0 + b, 0, 0)),
            _const_spec((1, d)),
            _const_spec(w_in.shape),
            _const_spec(caw.shape),
            _const_spec(cbw.shape),
            _const_spec((1, wb)),
            _const_spec((1, wb)),
            _const_spec((1, wb)),
            _const_spec((1, wa)),
            _const_spec((1, wb)),
            _const_spec(w_out.shape),
        ],
        out_specs=[
            pl.BlockSpec((None, tm, d), lambda b, i: (b, i, 0)),
            pl.BlockSpec((None, pad_a, wa), lambda b, i: (b, 0, 0)),
            pl.BlockSpec((None, pad_b, wb), lambda b, i: (b, 0, 0)),
        ],
        out_shape=[
            jax.ShapeDtypeStruct((n, l, d), F32),
            jax.ShapeDtypeStruct((n, pad_a, wa), F32),
            jax.ShapeDtypeStruct((n, pad_b, wb), F32),
        ],
        scratch_shapes=[
            pltpu.VMEM((pad_a + tm, wa), F32),
            pltpu.VMEM((pad_b + tm, wb), F32),
            pltpu.VMEM((tm, wb), F32),
        ],
        compiler_params=pltpu.CompilerParams(
            dimension_semantics=("parallel", "arbitrary"), vmem_limit_bytes=VMEM_LIMIT_BYTES),
        name="mixer",
    )(x, ha, hb, row(gmix), w_in, caw, cbw, row(cbb), row(cng), row(cnb), row(ga), row(gb), w_out)
    return h, na[:, pad_a - (ka - 1):], nb[:, pad_b - (kb - 1):]


def _top_rows(s, k):
    n, lanes = s.shape
    rows = lax.broadcasted_iota(I32, s.shape, 0)
    slot = lax.broadcasted_iota(I32, (k, lanes), 0)
    vals = jnp.zeros((k, lanes), F32)
    ids = jnp.zeros((k, lanes), I32)
    for it in range(k):
        m = jnp.max(s, axis=0, keepdims=True)
        am = jnp.min(jnp.where(s == m, rows, n), axis=0, keepdims=True)
        vals = jnp.where(slot == it, m, vals)
        ids = jnp.where(slot == it, am, ids)
        s = jnp.where(rows == am, -jnp.inf, s)
    return vals, ids


def _combine(v1, i1, v2, i2, n_keys):
    k, lanes = v1.shape
    half = k // 2
    vals, pos, eid = [], [], []

    def add(a_lo, a_n, b_lo, b_n):
        vals.append(v1[a_lo:a_lo + a_n] + v2[b_lo:b_lo + b_n])
        eid.append(i1[a_lo:a_lo + a_n] * n_keys + i2[b_lo:b_lo + b_n])
        r = lax.broadcasted_iota(I32, (max(a_n, b_n), lanes), 0)
        pos.append((a_lo + r) * k + b_lo if a_n > 1 else a_lo * k + b_lo + r)

    add(0, 1, 0, k)
    for a in range(1, half):
        add(a, 1, 0, half)
    add(half, k - half, 0, 1)
    cand = jnp.concatenate(vals, axis=0)
    pos = jnp.concatenate(pos, axis=0)
    eid = jnp.concatenate(eid, axis=0)

    slot = lax.broadcasted_iota(I32, (k, lanes), 0)
    top_s = jnp.zeros((k, lanes), F32)
    top_e = jnp.zeros((k, lanes), I32)
    for it in range(k):
        m = jnp.max(cand, axis=0, keepdims=True)
        sel = jnp.min(jnp.where(cand == m, pos, k * k), axis=0, keepdims=True)
        hit = pos == sel
        e = jnp.max(jnp.where(hit, eid, -1), axis=0, keepdims=True)
        top_s = jnp.where(slot == it, m, top_s)
        top_e = jnp.where(slot == it, e, top_e)
        cand = jnp.where(hit, -jnp.inf, cand)
    return top_s, top_e


def _route_body(h_ref, g_ref, wq_ref, sk_ref, xn_ref, idx_ref, gate_ref, qh_s, ql_s, st_s, et_s, gt_s,
                *, tm, heads, n_keys, d_key):
    hn = _rms(h_ref[...], g_ref[...])
    xn_ref[...] = hn
    qh_s[...], ql_s[...] = _hi_lo(_dot3(hn, wq_ref[0], wq_ref[1]))
    dot_nt = functools.partial(lax.dot_general, dimension_numbers=(((1,), (1,)), ((), ())),
                               preferred_element_type=F32)
    for hp in range(2 * heads):
        cols = slice(hp * d_key, (hp + 1) * d_key)
        st_s[hp] = ((dot_nt(sk_ref[0, hp], ql_s[:, cols]) + dot_nt(sk_ref[1, hp], qh_s[:, cols]))
                    + dot_nt(sk_ref[0, hp], qh_s[:, cols]))
    n_chunk = tm // LANES

    def per_head_chunk(j, carry):
        hd = j // n_chunk
        col = pl.multiple_of((j % n_chunk) * LANES, LANES)
        v1, i1 = _top_rows(st_s[2 * hd, :, pl.ds(col, LANES)], PEER_TOPK)
        v2, i2 = _top_rows(st_s[2 * hd + 1, :, pl.ds(col, LANES)], PEER_TOPK)
        top_s, top_e = _combine(v1, i1, v2, i2, n_keys)
        p = jnp.exp(top_s - top_s[0:1])
        gate = p / jnp.sum(p, axis=0, keepdims=True)
        row = pl.multiple_of(hd * PEER_TOPK, PEER_TOPK)
        et_s[pl.ds(row, PEER_TOPK), pl.ds(col, LANES)] = top_e
        gt_s[pl.ds(row, PEER_TOPK), pl.ds(col, LANES)] = gate
        return carry

    lax.fori_loop(0, heads * n_chunk, per_head_chunk, 0)
    idx_ref[...] = et_s[...].T
    gate_ref[...] = gt_s[...].T


def _route(h2, tok0, t, g, w_q, sk):
    d = h2.shape[1]
    _, heads2, n_keys, d_key = sk.shape
    heads = heads2 // 2
    hk = heads * PEER_TOPK
    tm = min(t, 512)
    assert t % tm == 0 and tok0 % tm == 0 and tm % LANES == 0 and d_key == LANES and n_keys % SUBLANES == 0
    first = tok0 // tm
    body = functools.partial(_route_body, tm=tm, heads=heads, n_keys=n_keys, d_key=d_key)
    return pl.pallas_call(
        body,
        grid=(t // tm,),
        in_specs=[
            pl.BlockSpec((tm, d), lambda i: (first + i, 0)),
            _const_spec((1, d)),
            _const_spec(w_q.shape),
            _const_spec(sk.shape),
        ],
        out_specs=[
            pl.BlockSpec((tm, d), lambda i: (i, 0)),
            pl.BlockSpec((tm, hk), lambda i: (i, 0)),
            pl.BlockSpec((tm, hk), lambda i: (i, 0)),
        ],
        out_shape=[
            jax.ShapeDtypeStruct((t, d), F32),
            jax.ShapeDtypeStruct((t, hk), I32),
            jax.ShapeDtypeStruct((t, hk), F32),
        ],
        scratch_shapes=[
            pltpu.VMEM((tm, 2 * heads * d_key), BF16),
            pltpu.VMEM((tm, 2 * heads * d_key), BF16),
            pltpu.VMEM((2 * heads, n_keys, tm), F32),
            pltpu.VMEM((hk, tm), I32),
            pltpu.VMEM((hk, tm), F32),
        ],
        compiler_params=pltpu.CompilerParams(
            dimension_semantics=("parallel",), vmem_limit_bytes=VMEM_LIMIT_BYTES),
        name="route",
    )(h2, g.reshape(1, d), w_q, sk)


GATHER_ROWS = 16
GATHER_BUFFERS = 6
LANE_STEPS = 4
BLOCK_LANES = LANE_STEPS * SC_LANES
BLOCKS_PER_TILE = LANES // BLOCK_LANES
MIX_LANE_STEPS = 2
MIX_BLOCK_LANES = MIX_LANE_STEPS * SC_LANES
MIX_BLOCKS_PER_TILE = LANES // MIX_BLOCK_LANES
GROUP_TOKENS = 8


def _sc_mesh():
    return plsc.VectorSubcoreMesh(core_axis_name="c", subcore_axis_name="s")


def _sc_geometry(t, hk):
    per_worker = t // SC_WORKERS
    group = min(per_worker, GROUP_TOKENS)
    chunks = hk // GATHER_ROWS
    assert t % SC_WORKERS == 0 and per_worker % group == 0 and group % SUBLANES == 0
    assert hk % GATHER_ROWS == 0 and GATHER_BUFFERS <= chunks and GATHER_ROWS == SC_LANES
    return per_worker, group, chunks


def _token_stream(first_token, per_worker, group, chunks, idx_hbm, side_hbm, table_hbm, out_hbm,
                  idx_v, side_v, o_v, ring, sems, stage_sems, compute):
    n_groups = per_worker // group

    def stage(grp, par):
        tok0 = pl.multiple_of(first_token + grp * group, SUBLANES)
        return (pltpu.make_async_copy(idx_hbm.at[pl.ds(tok0, group)], idx_v.at[par], stage_sems.at[par]),
                pltpu.make_async_copy(side_hbm.at[pl.ds(tok0, group)], side_v.at[par], stage_sems.at[par]))

    def gather(par, g, c, slot):
        rows = idx_v[par, g, pl.ds(c * GATHER_ROWS, GATHER_ROWS)]
        return pltpu.make_async_copy(table_hbm.at[rows], ring.at[slot], sems.at[slot])

    for cp in stage(0, 0):
        cp.start()
    for cp in stage(0, 0):
        cp.wait()
    for c in range(GATHER_BUFFERS):
        gather(0, 0, c, c).start()

    @pl.loop(0, per_worker)
    def _(tk):
        grp = tk // group
        g = tk % group
        par = grp % 2
        more_groups = grp + 1 < n_groups

        @pl.when((g == 0) & more_groups)
        def _():
            for cp in stage(grp + 1, 1 - par):
                cp.start()

        @pl.when((g == group - 1) & more_groups)
        def _():
            for cp in stage(grp + 1, 1 - par):
                cp.wait()

        for c in range(chunks):
            slot = lax.rem(tk * chunks + c, GATHER_BUFFERS)
            gather(par, g, c, slot).wait()
            compute(par, g, c, slot)
            nxt = c + GATHER_BUFFERS
            if nxt < chunks:
                gather(par, g, nxt, slot).start()
            else:
                @pl.when(tk + 1 < per_worker)
                def _():
                    gather(((tk + 1) // group) % 2, (tk + 1) % group, nxt - chunks, slot).start()

        @pl.when(g == group - 1)
        def _():
            tok0 = pl.multiple_of(first_token + grp * group, SUBLANES)
            pltpu.sync_copy(o_v, out_hbm.at[pl.ds(tok0, group)])


def _row_tiles(table):
    e, d = table.shape
    assert d % LANES == 0
    return table.reshape(e, d // LANES, LANES)


def _peer_dots(xn, idx, table, after):
    t, d = xn.shape
    hk = idx.shape[1]
    per_worker, group, chunks = _sc_geometry(t, hk)
    tiles = table.shape[1]

    def body(xn_hbm, idx_hbm, table_hbm, after_hbm, out_hbm, idx_v, x_v, o_v, ring, sems, stage_sems):
        del after_hbm
        wid = lax.axis_index("c") * SC_SUBCORES + lax.axis_index("s")
        lane = lax.iota(I32, SC_LANES)

        def compute(par, g, c, slot):
            def col_block(cb, accs):
                tile, lane0 = cb // BLOCKS_PER_TILE, (cb % BLOCKS_PER_TILE) * BLOCK_LANES
                for u in range(LANE_STEPS):
                    xc = x_v[par, g, pl.ds(cb * BLOCK_LANES + u * SC_LANES, SC_LANES)]
                    accs = tuple(a + ring[slot, r, tile, pl.ds(lane0 + u * SC_LANES, SC_LANES)] * xc
                                 for r, a in enumerate(accs))
                return accs

            zero = jnp.zeros((SC_LANES,), F32)
            accs = lax.fori_loop(0, tiles * BLOCKS_PER_TILE, col_block, (zero,) * GATHER_ROWS)
            tot = zero
            for r in range(GATHER_ROWS):
                tot = jnp.where(lane == r, jnp.sum(accs[r]), tot)
            o_v[g, pl.ds(c * GATHER_ROWS, GATHER_ROWS)] = tot

        _token_stream(wid * per_worker, per_worker, group, chunks, idx_hbm, xn_hbm, table_hbm, out_hbm,
                      idx_v, x_v, o_v, ring, sems, stage_sems, compute)

    return pl.kernel(
        body,
        out_type=jax.ShapeDtypeStruct((t, hk), F32),
        mesh=_sc_mesh(),
        scratch_types=[
            pltpu.VMEM((2, group, hk), I32),
            pltpu.VMEM((2, group, d), F32),
            pltpu.VMEM((group, hk), F32),
            pltpu.VMEM((GATHER_BUFFERS, GATHER_ROWS, tiles, LANES), F32),
            pltpu.SemaphoreType.DMA((GATHER_BUFFERS,)),
            pltpu.SemaphoreType.DMA((2,)),
        ],
        compiler_params=pltpu.CompilerParams(needs_layout_passes=False),
        name="peer_dots",
    )(xn, idx, table, after)


def _peer_mix(w, idx, table):
    t, hk = w.shape
    tiles = table.shape[1]
    d = tiles * LANES
    per_worker, group, chunks = _sc_geometry(t, hk)

    def body(w_hbm, idx_hbm, table_hbm, out_hbm, idx_v, w_v, o_v, ring, sems, stage_sems, acc_v):
        wid = lax.axis_index("c") * SC_SUBCORES + lax.axis_index("s")

        def compute(par, g, c, slot):
            wv = w_v[par, g, pl.ds(c * GATHER_ROWS, GATHER_ROWS)]
            ws = [jnp.full((SC_LANES,), wv[r], F32) for r in range(GATHER_ROWS)]

            @plsc.parallel_loop(0, tiles * MIX_BLOCKS_PER_TILE)
            def _(cb):
                tile, lane0 = cb // MIX_BLOCKS_PER_TILE, (cb % MIX_BLOCKS_PER_TILE) * MIX_BLOCK_LANES
                sums = []
                for u in range(MIX_LANE_STEPS):
                    part = [None] * 4
                    for r in range(GATHER_ROWS):
                        term = ws[r] * ring[slot, r, tile, pl.ds(lane0 + u * SC_LANES, SC_LANES)]
                        part[r % 4] = term if part[r % 4] is None else part[r % 4] + term
                    sums.append((part[0] + part[1]) + (part[2] + part[3]))
                for u in range(MIX_LANE_STEPS):
                    col = cb * MIX_BLOCK_LANES + u * SC_LANES
                    if c == 0:
                        acc_v[pl.ds(col, SC_LANES)] = sums[u]
                    elif c < chunks - 1:
                        acc_v[pl.ds(col, SC_LANES)] = acc_v[pl.ds(col, SC_LANES)] + sums[u]
                    else:
                        o_v[g, pl.ds(col, SC_LANES)] = acc_v[pl.ds(col, SC_LANES)] + sums[u]

        _token_stream(wid * per_worker, per_worker, group, chunks, idx_hbm, w_hbm, table_hbm, out_hbm,
                      idx_v, w_v, o_v, ring, sems, stage_sems, compute)

    return pl.kernel(
        body,
        out_type=jax.ShapeDtypeStruct((t, d), F32),
        mesh=_sc_mesh(),
        scratch_types=[
            pltpu.VMEM((2, group, hk), I32),
            pltpu.VMEM((2, group, hk), F32),
            pltpu.VMEM((group, d), F32),
            pltpu.VMEM((GATHER_BUFFERS, GATHER_ROWS, tiles, LANES), F32),
            pltpu.SemaphoreType.DMA((GATHER_BUFFERS,)),
            pltpu.SemaphoreType.DMA((2,)),
            pltpu.VMEM((d,), F32),
        ],
        compiler_params=pltpu.CompilerParams(needs_layout_passes=False),
        name="peer_mix",
    )(w, idx, table)


def _gate_body(d_ref, gate_ref, w_ref):
    x = d_ref[...]
    sqrt_half = np.sqrt(0.5).astype(np.float32)
    w_ref[...] = gate_ref[...] * (0.5 * x * (1.0 + lax.erf(x * sqrt_half)))


def _gate(dots, gate):
    t, hk = dots.shape
    tm = min(t, 1024)
    assert t % tm == 0
    spec = pl.BlockSpec((tm, hk), lambda i: (i, 0))
    return pl.pallas_call(
        _gate_body, grid=(t // tm,), in_specs=[spec, spec], out_specs=spec,
        out_shape=jax.ShapeDtypeStruct((t, hk), F32),
        compiler_params=pltpu.CompilerParams(dimension_semantics=("parallel",)),
        name="gate",
    )(dots, gate)


def _residual_body(y_all_ref, h_ref, p_ref, g_ref, y_ref, *, final):
    del y_all_ref
    x = h_ref[...] + p_ref[...]
    y_ref[...] = _rms(x, g_ref[...]) if final else x


def _residual(y_all, y0, h2, h0, p, g, final):
    t, d = p.shape
    tm = min(t, 1024)
    assert t % tm == 0 and y0 % tm == 0 and h0 % tm == 0
    y_first, h_first = y0 // tm, h0 // tm
    return pl.pallas_call(
        functools.partial(_residual_body, final=final),
        grid=(t // tm,),
        in_specs=[pl.BlockSpec(memory_space=pl.ANY),
                  pl.BlockSpec((tm, d), lambda i: (h_first + i, 0)),
                  pl.BlockSpec((tm, d), lambda i: (i, 0)),
                  _const_spec((1, d))],
        out_specs=pl.BlockSpec((tm, d), lambda i: (y_first + i, 0)),
        out_shape=jax.ShapeDtypeStruct(y_all.shape, F32),
        input_output_aliases={0: 0},
        compiler_params=pltpu.CompilerParams(dimension_semantics=("parallel",)),
        name="residual",
    )(y_all, h2, p, g.reshape(1, d))


SEQ_PARTS = 8


def _num_parts(n, l):
    per_part = (n // SEQ_PARTS) * l
    ok = n % SEQ_PARTS == 0 and per_part % (SC_WORKERS * SUBLANES) == 0
    return SEQ_PARTS if ok else 1


MIX_LAG = 2
FIRST_PIECE_TOKENS = 1024


def _token_pieces(q, tokens):
    quantum = SC_WORKERS * GROUP_TOKENS
    rest = tokens - FIRST_PIECE_TOKENS
    if q == 0 and rest > 0 and FIRST_PIECE_TOKENS % quantum == 0 and rest % quantum == 0:
        return [(0, FIRST_PIECE_TOKENS), (FIRST_PIECE_TOKENS, rest)]
    return [(0, tokens)]


def _trunk(x, hist_a, hist_b, norm_mix_g, w_in, conv_a_w, conv_b_w, conv_b_b, conv_norm_g, conv_norm_b,
           out_norm_a_g, out_norm_b_g, w_out, norm_ffn_g, w_q, sub_keys, u_tiles, v_tiles, final_norm_g):
    n, l, d = x.shape
    depth = len(w_in)
    parts = _num_parts(n, l)
    seqs = n // parts
    new_a, new_b = [], []
    for layer in range(depth):
        last = layer == depth - 1
        y_all = jnp.zeros((n * l, d), F32)
        nas, nbs, mixed = [], [], []
        for q in range(parts):
            h, na, nb = _mixer(x, q * seqs, seqs, hist_a[layer], hist_b[layer], norm_mix_g[layer], w_in[layer],
                               conv_a_w[layer], conv_b_w[layer], conv_b_b[layer], conv_norm_g[layer],
                               conv_norm_b[layer], out_norm_a_g[layer], out_norm_b_g[layer], w_out[layer])
            h2 = h.reshape(seqs * l, d)
            for t0, tn in _token_pieces(q, seqs * l):
                xn, idx, gate = _route(h2, t0, tn, norm_ffn_g[layer], w_q[layer], sub_keys[layer])
                after = mixed[-MIX_LAG] if len(mixed) >= MIX_LAG else idx
                dots = _peer_dots(xn, idx, u_tiles[layer], after)
                w = _gate(dots, gate)
                p = _peer_mix(w, idx, v_tiles[layer])
                mixed.append(p)
                y_all = _residual(y_all, q * seqs * l + t0, h2, t0, p,
                                  final_norm_g if last else jnp.ones((d,), F32), final=last)
            nas.append(na)
            nbs.append(nb)
        x = y_all.reshape(n, l, d)
        new_a.append(jnp.concatenate(nas, axis=0))
        new_b.append(jnp.concatenate(nbs, axis=0))
    return x, jnp.stack(new_a), jnp.stack(new_b)


def kernel(x_prompt, x_sample, cache_conv_a, cache_conv_b, norm_mix_g, w_in, conv_a_w, conv_b_w, conv_b_b,
           conv_norm_g, conv_norm_b, out_norm_a_g, out_norm_b_g, w_out, norm_ffn_g, w_q, sub_keys,
           u_experts, v_experts, final_norm_g):
    depth = w_in.shape[0]
    _, heads, _, n_keys, d_key = sub_keys.shape
    u_tiles = [_row_tiles(u_experts[layer]) for layer in range(depth)]
    v_tiles = [_row_tiles(v_experts[layer]) for layer in range(depth)]
    w_in_s = [_split(w_in[layer]) for layer in range(depth)]
    w_out_s = [_split(w_out[layer]) for layer in range(depth)]
    w_q_s = [_split(w_q[layer]) for layer in range(depth)]
    sk_s = [_split(sub_keys[layer].reshape(heads * 2 * n_keys, d_key)).reshape(2, heads * 2, n_keys, d_key)
            for layer in range(depth)]
    weights = (norm_mix_g, w_in_s, conv_a_w, conv_b_w, conv_b_b, conv_norm_g, conv_norm_b, out_norm_a_g,
               out_norm_b_g, w_out_s, norm_ffn_g, w_q_s, sk_s, u_tiles, v_tiles, final_norm_g)
    n = x_prompt.shape[0]
    y_sample, conv_a_sample, conv_b_sample = _trunk(x_sample, cache_conv_a, cache_conv_b, *weights)
    zeros_a = jnp.zeros((depth, n) + cache_conv_a.shape[2:], x_prompt.dtype)
    zeros_b = jnp.zeros((depth, n) + cache_conv_b.shape[2:], x_prompt.dtype)
    y_prompt, conv_a_prompt, conv_b_prompt = _trunk(x_prompt, zeros_a, zeros_b, *weights)
    return (y_prompt, y_sample, conv_a_prompt, conv_b_prompt, conv_a_sample, conv_b_sample)
```

```python
import functools

import numpy as np
import jax
import jax.numpy as jnp
from jax import lax
from jax.experimental import pallas as pl
from jax.experimental.pallas import tpu as pltpu
from jax.experimental.pallas import tpu_sc as plsc

F32 = jnp.float32
I32 = jnp.int32
BF16 = jnp.bfloat16
EPS = 1e-6

LANES = 128
SUBLANES = 8
VMEM_LIMIT_BYTES = 56 * 1024 * 1024
ROW_TILE = 512
LIGHT_ROW_TILE = 1024
SPLIT_ROW_TILE = 256
SC_CORES = 2
SC_SUBCORES = 16
SC_LANES = 16
SC_WORKERS = SC_CORES * SC_SUBCORES

PEER_TOPK = 16


def _rms(x, g):
    return x * lax.rsqrt(jnp.mean(x * x, axis=-1, keepdims=True) + EPS) * g


def _const_spec(shape):
    zeros = (0,) * len(shape)
    return pl.BlockSpec(shape, lambda *_: zeros, pipeline_mode=pl.Buffered(1))


def _hi_lo(x):
    hi = x.astype(BF16)
    return hi, (x - hi.astype(F32)).astype(BF16)


def _dot3(a, b_hi, b_lo, dims=(((1,), (0,)), ((), ()))):
    a_hi, a_lo = _hi_lo(a)
    dot = functools.partial(lax.dot_general, dimension_numbers=dims, preferred_element_type=F32)
    return (dot(a_hi, b_lo) + dot(a_lo, b_hi)) + dot(a_hi, b_hi)


def _split_body(w_ref, s_ref):
    hi, lo = _hi_lo(w_ref[...])
    s_ref[0] = hi
    s_ref[1] = lo


def _split(w):
    r, c = w.shape
    tm = min(r, SPLIT_ROW_TILE)
    assert r % tm == 0
    return pl.pallas_call(
        _split_body, grid=(r // tm,),
        in_specs=[pl.BlockSpec((tm, c), lambda i: (i, 0))],
        out_specs=pl.BlockSpec((2, tm, c), lambda i: (0, i, 0)),
        out_shape=jax.ShapeDtypeStruct((2, r, c), BF16),
        compiler_params=pltpu.CompilerParams(dimension_semantics=("parallel",)),
        name="split",
    )(w)


CONV_ROW_BLOCK = 64


def _mixer_body(x_ref, ha_ref, hb_ref, gmix_ref, win_ref, caw_ref, cbw_ref, cbb_ref, cng_ref, cnb_ref,
                ga_ref, gb_ref, wout_ref, h_ref, na_ref, nb_ref, xa_s, xb_s, cb_s,
                *, tm, wa, wb, ka, kb, pad_a, pad_b):
    i = pl.program_id(1)

    @pl.when(i == 0)
    def _():
        xa_s[0:pad_a] = ha_ref[...]
        xb_s[0:pad_b] = hb_ref[...]

    x = x_ref[...]
    xn = _rms(x, gmix_ref[...])
    proj = _dot3(xn, win_ref[0], win_ref[1])
    h_a = proj[:, 0:wa]
    c_a = proj[:, wa:2 * wa]
    b_a = proj[:, 2 * wa:3 * wa]
    v_b = proj[:, 3 * wa:3 * wa + wb]
    g_b = proj[:, 3 * wa + wb:3 * wa + 2 * wb]

    xa_s[pad_a:pad_a + tm] = c_a * h_a
    conv_a = caw_ref[0:1, :] * xa_s[pad_a - (ka - 1):pad_a - (ka - 1) + tm]
    for k in range(1, ka):
        lo = pad_a - (ka - 1) + k
        conv_a = conv_a + caw_ref[k:k + 1, :] * xa_s[lo:lo + tm]
    y_a = _rms(b_a * conv_a, ga_ref[...])

    xb_s[pad_b:pad_b + tm] = v_b * jax.nn.sigmoid(g_b)
    rb = min(tm, CONV_ROW_BLOCK)
    for r0 in range(0, tm, rb):
        base = r0 + pad_b - (kb - 1)
        acc = cbw_ref[0:1, :] * xb_s[base:base + rb]
        for k in range(1, kb):
            acc = acc + cbw_ref[k:k + 1, :] * xb_s[base + k:base + k + rb]
        cb_s[r0:r0 + rb] = acc + cbb_ref[...]
    cb = cb_s[...]
    mu = jnp.mean(cb, axis=-1, keepdims=True)
    xc = cb - mu
    ln = xc * lax.rsqrt(jnp.mean(xc * xc, axis=-1, keepdims=True) + EPS) * cng_ref[...] + cnb_ref[...]
    z = ln * jax.nn.sigmoid(ln)
    y_b = _rms(z, gb_ref[...])

    y = (_dot3(y_a, wout_ref[0, 0:wa, :], wout_ref[1, 0:wa, :])
         + _dot3(y_b, wout_ref[0, wa:wa + wb, :], wout_ref[1, wa:wa + wb, :]))
    h_ref[...] = x + y

    xa_s[0:pad_a] = xa_s[tm:tm + pad_a]
    xb_s[0:pad_b] = xb_s[tm:tm + pad_b]

    @pl.when(i == pl.num_programs(1) - 1)
    def _():
        na_ref[...] = xa_s[0:pad_a]
        nb_ref[...] = xb_s[0:pad_b]


def _mixer(x, seq0, n, hist_a, hist_b, gmix, w_in, caw, cbw, cbb, cng, cnb, ga, gb, w_out):
    _, l, d = x.shape
    ka, wa = caw.shape
    kb, wb = cbw.shape
    pad_a = SUBLANES
    pad_b = -(-(kb - 1) // SUBLANES) * SUBLANES
    tm = min(l, ROW_TILE)
    assert l % tm == 0 and tm % min(tm, CONV_ROW_BLOCK) == 0
    assert tm >= pad_b and tm % SUBLANES == 0 and ka - 1 <= pad_a
    ha = jnp.pad(hist_a, ((0, 0), (pad_a - (ka - 1), 0), (0, 0)))
    hb = jnp.pad(hist_b, ((0, 0), (pad_b - (kb - 1), 0), (0, 0)))
    row = lambda v: v.reshape(1, -1)
    body = functools.partial(_mixer_body, tm=tm, wa=wa, wb=wb, ka=ka, kb=kb, pad_a=pad_a, pad_b=pad_b)
    h, na, nb = pl.pallas_call(
        body,
        grid=(n, l // tm),
        in_specs=[
            pl.BlockSpec((None, tm, d), lambda b, i: (seq0 + b, i, 0)),
            pl.BlockSpec((None, pad_a, wa), lambda b, i: (seq0 + b, 0, 0)),
            pl.BlockSpec((None, pad_b, wb), lambda b, i: (seq0 + b, 0, 0)),
            _const_spec((1, d)),
            _const_spec(w_in.shape),
            _const_spec(caw.shape),
            _const_spec(cbw.shape),
            _const_spec((1, wb)),
            _const_spec((1, wb)),
            _const_spec((1, wb)),
            _const_spec((1, wa)),
            _const_spec((1, wb)),
            _const_spec(w_out.shape),
        ],
        out_specs=[
            pl.BlockSpec((None, tm, d), lambda b, i: (b, i, 0)),
            pl.BlockSpec((None, pad_a, wa), lambda b, i: (b, 0, 0)),
            pl.BlockSpec((None, pad_b, wb), lambda b, i: (b, 0, 0)),
        ],
        out_shape=[
            jax.ShapeDtypeStruct((n, l, d), F32),
            jax.ShapeDtypeStruct((n, pad_a, wa), F32),
            jax.ShapeDtypeStruct((n, pad_b, wb), F32),
        ],
        scratch_shapes=[
            pltpu.VMEM((pad_a + tm, wa), F32),
            pltpu.VMEM((pad_b + tm, wb), F32),
            pltpu.VMEM((tm, wb), F32),
        ],
        compiler_params=pltpu.CompilerParams(
            dimension_semantics=("parallel", "arbitrary"), vmem_limit_bytes=VMEM_LIMIT_BYTES),
        name="mixer",
    )(x, ha, hb, row(gmix), w_in, caw, cbw, row(cbb), row(cng), row(cnb), row(ga), row(gb), w_out)
    return h, na[:, pad_a - (ka - 1):], nb[:, pad_b - (kb - 1):]


def _top_rows(s, k):
    n, lanes = s.shape
    rows = lax.broadcasted_iota(I32, s.shape, 0)
    slot = lax.broadcasted_iota(I32, (k, lanes), 0)
    vals = jnp.zeros((k, lanes), F32)
    ids = jnp.zeros((k, lanes), I32)
    for it in range(k):
        m = jnp.max(s, axis=0, keepdims=True)
        am = jnp.min(jnp.where(s == m, rows, n), axis=0, keepdims=True)
        vals = jnp.where(slot == it, m, vals)
        ids = jnp.where(slot == it, am, ids)
        s = jnp.where(rows == am, -jnp.inf, s)
    return vals, ids


def _combine(v1, i1, v2, i2, n_keys):
    k, lanes = v1.shape
    half = k // 2
    vals, pos, eid = [], [], []

    def add(a_lo, a_n, b_lo, b_n):
        vals.append(v1[a_lo:a_lo + a_n] + v2[b_lo:b_lo + b_n])
        eid.append(i1[a_lo:a_lo + a_n] * n_keys + i2[b_lo:b_lo + b_n])
        r = lax.broadcasted_iota(I32, (max(a_n, b_n), lanes), 0)
        pos.append((a_lo + r) * k + b_lo if a_n > 1 else a_lo * k + b_lo + r)

    add(0, 1, 0, k)
    for a in range(1, half):
        add(a, 1, 0, half)
    add(half, k - half, 0, 1)
    cand = jnp.concatenate(vals, axis=0)
    pos = jnp.concatenate(pos, axis=0)
    eid = jnp.concatenate(eid, axis=0)

    slot = lax.broadcasted_iota(I32, (k, lanes), 0)
    top_s = jnp.zeros((k, lanes), F32)
    top_e = jnp.zeros((k, lanes), I32)
    for it in range(k):
        m = jnp.max(cand, axis=0, keepdims=True)
        sel = jnp.min(jnp.where(cand == m, pos, k * k), axis=0, keepdims=True)
        hit = pos == sel
        e = jnp.max(jnp.where(hit, eid, -1), axis=0, keepdims=True)
        top_s = jnp.where(slot == it, m, top_s)
        top_e = jnp.where(slot == it, e, top_e)
        cand = jnp.where(hit, -jnp.inf, cand)
    return top_s, top_e


def _route_body(h_ref, g_ref, wq_ref, sk_ref, xn_ref, idx_ref, gate_ref, qh_s, ql_s, st_s, et_s, gt_s,
                *, tm, heads, n_keys, d_key):
    hn = _rms(h_ref[...], g_ref[...])
    xn_ref[...] = hn
    qh_s[...], ql_s[...] = _hi_lo(_dot3(hn, wq_ref[0], wq_ref[1]))
    dot_nt = functools.partial(lax.dot_general, dimension_numbers=(((1,), (1,)), ((), ())),
                               preferred_element_type=F32)
    for hp in range(2 * heads):
        cols = slice(hp * d_key, (hp + 1) * d_key)
        st_s[hp] = ((dot_nt(sk_ref[0, hp], ql_s[:, cols]) + dot_nt(sk_ref[1, hp], qh_s[:, cols]))
                    + dot_nt(sk_ref[0, hp], qh_s[:, cols]))
    n_chunk = tm // LANES

    def per_head_chunk(j, carry):
        hd = j // n_chunk
        col = pl.multiple_of((j % n_chunk) * LANES, LANES)
        v1, i1 = _top_rows(st_s[2 * hd, :, pl.ds(col, LANES)], PEER_TOPK)
        v2, i2 = _top_rows(st_s[2 * hd + 1, :, pl.ds(col, LANES)], PEER_TOPK)
        top_s, top_e = _combine(v1, i1, v2, i2, n_keys)
        p = jnp.exp(top_s - top_s[0:1])
        gate = p / jnp.sum(p, axis=0, keepdims=True)
        row = pl.multiple_of(hd * PEER_TOPK, PEER_TOPK)
        et_s[pl.ds(row, PEER_TOPK), pl.ds(col, LANES)] = top_e
        gt_s[pl.ds(row, PEER_TOPK), pl.ds(col, LANES)] = gate
        return carry

    lax.fori_loop(0, heads * n_chunk, per_head_chunk, 0)
    idx_ref[...] = et_s[...].T
    gate_ref[...] = gt_s[...].T


def _route(h2, tok0, t, g, w_q, sk):
    d = h2.shape[1]
    _, heads2, n_keys, d_key = sk.shape
    heads = heads2 // 2
    hk = heads * PEER_TOPK
    tm = min(t, ROW_TILE)
    assert t % tm == 0 and tok0 % tm == 0 and tm % LANES == 0 and d_key == LANES and n_keys % SUBLANES == 0
    first = tok0 // tm
    body = functools.partial(_route_body, tm=tm, heads=heads, n_keys=n_keys, d_key=d_key)
    return pl.pallas_call(
        body,
        grid=(t // tm,),
        in_specs=[
            pl.BlockSpec((tm, d), lambda i: (first + i, 0)),
            _const_spec((1, d)),
            _const_spec(w_q.shape),
            _const_spec(sk.shape),
        ],
        out_specs=[
            pl.BlockSpec((tm, d), lambda i: (i, 0)),
            pl.BlockSpec((tm, hk), lambda i: (i, 0)),
            pl.BlockSpec((tm, hk), lambda i: (i, 0)),
        ],
        out_shape=[
            jax.ShapeDtypeStruct((t, d), F32),
            jax.ShapeDtypeStruct((t, hk), I32),
            jax.ShapeDtypeStruct((t, hk), F32),
        ],
        scratch_shapes=[
            pltpu.VMEM((tm, 2 * heads * d_key), BF16),
            pltpu.VMEM((tm, 2 * heads * d_key), BF16),
            pltpu.VMEM((2 * heads, n_keys, tm), F32),
            pltpu.VMEM((hk, tm), I32),
            pltpu.VMEM((hk, tm), F32),
        ],
        compiler_params=pltpu.CompilerParams(
            dimension_semantics=("parallel",), vmem_limit_bytes=VMEM_LIMIT_BYTES),
        name="route",
    )(h2, g.reshape(1, d), w_q, sk)


GATHER_ROWS = 16
GATHER_BUFFERS = 4
LANE_STEPS = 4
BLOCK_LANES = LANE_STEPS * SC_LANES
BLOCKS_PER_TILE = LANES // BLOCK_LANES
MIX_LANE_STEPS = 2
MIX_BLOCK_LANES = MIX_LANE_STEPS * SC_LANES
MIX_BLOCKS_PER_TILE = LANES // MIX_BLOCK_LANES
GROUP_TOKENS = 16


def _sc_mesh():
    return plsc.VectorSubcoreMesh(core_axis_name="c", subcore_axis_name="s")


def _sc_geometry(t, hk):
    per_worker = t // SC_WORKERS
    group = min(per_worker, GROUP_TOKENS)
    chunks = hk // GATHER_ROWS
    assert t % SC_WORKERS == 0 and per_worker % group == 0 and group % SUBLANES == 0
    assert hk % GATHER_ROWS == 0 and chunks % GATHER_BUFFERS == 0 and GATHER_ROWS == SC_LANES
    return per_worker, group, chunks


def _token_stream(first_token, per_worker, group, chunks, idx_hbm, side_hbm, table_hbm, out_hbm,
                  idx_v, side_v, o_v, ring, sems, stage_sems, compute):
    n_groups = per_worker // group

    def stage(grp, par):
        tok0 = pl.multiple_of(first_token + grp * group, SUBLANES)
        return (pltpu.make_async_copy(idx_hbm.at[pl.ds(tok0, group)], idx_v.at[par], stage_sems.at[par]),
                pltpu.make_async_copy(side_hbm.at[pl.ds(tok0, group)], side_v.at[par], stage_sems.at[par]))

    def gather(par, g, c, slot):
        rows = idx_v[par, g, pl.ds(c * GATHER_ROWS, GATHER_ROWS)]
        return pltpu.make_async_copy(table_hbm.at[rows], ring.at[slot], sems.at[slot])

    for cp in stage(0, 0):
        cp.start()
    for cp in stage(0, 0):
        cp.wait()
    for c in range(GATHER_BUFFERS):
        gather(0, 0, c, c).start()

    @pl.loop(0, per_worker)
    def _(tk):
        grp = tk // group
        g = tk % group
        par = grp % 2
        more_groups = grp + 1 < n_groups

        @pl.when((g == 0) & more_groups)
        def _():
            for cp in stage(grp + 1, 1 - par):
                cp.start()

        @pl.when((g == group - 1) & more_groups)
        def _():
            for cp in stage(grp + 1, 1 - par):
                cp.wait()

        for c in range(chunks):
            slot = c % GATHER_BUFFERS
            gather(par, g, c, slot).wait()
            compute(par, g, c, slot)
            nxt = c + GATHER_BUFFERS
            if nxt < chunks:
                gather(par, g, nxt, slot).start()
            else:
                @pl.when(tk + 1 < per_worker)
                def _():
                    gather(((tk + 1) // group) % 2, (tk + 1) % group, nxt - chunks, slot).start()

        @pl.when(g == group - 1)
        def _():
            tok0 = pl.multiple_of(first_token + grp * group, SUBLANES)
            pltpu.sync_copy(o_v, out_hbm.at[pl.ds(tok0, group)])


def _row_tiles(table):
    e, d = table.shape
    assert d % LANES == 0
    return table.reshape(e, d // LANES, LANES)


def _peer_dots(xn, idx, table, after, t_sc):
    t, d = xn.shape
    hk = idx.shape[1]
    per_worker, group, chunks = _sc_geometry(t_sc, hk)
    tiles = table.shape[1]

    def body(xn_hbm, idx_hbm, table_hbm, after_hbm, out_hbm, idx_v, x_v, o_v, ring, sems, stage_sems):
        del after_hbm
        wid = lax.axis_index("c") * SC_SUBCORES + lax.axis_index("s")
        lane = lax.iota(I32, SC_LANES)

        def compute(par, g, c, slot):
            def col_block(cb, accs):
                tile, lane0 = cb // BLOCKS_PER_TILE, (cb % BLOCKS_PER_TILE) * BLOCK_LANES
                for u in range(LANE_STEPS):
                    xc = x_v[par, g, pl.ds(cb * BLOCK_LANES + u * SC_LANES, SC_LANES)]
                    accs = tuple(a + ring[slot, r, tile, pl.ds(lane0 + u * SC_LANES, SC_LANES)] * xc
                                 for r, a in enumerate(accs))
                return accs

            zero = jnp.zeros((SC_LANES,), F32)
            accs = lax.fori_loop(0, tiles * BLOCKS_PER_TILE, col_block, (zero,) * GATHER_ROWS)
            tot = zero
            for r in range(GATHER_ROWS):
                tot = jnp.where(lane == r, jnp.sum(accs[r]), tot)
            o_v[g, pl.ds(c * GATHER_ROWS, GATHER_ROWS)] = tot

        _token_stream(wid * per_worker, per_worker, group, chunks, idx_hbm, xn_hbm, table_hbm, out_hbm,
                      idx_v, x_v, o_v, ring, sems, stage_sems, compute)

    return pl.kernel(
        body,
        out_type=jax.ShapeDtypeStruct((t, hk), F32),
        mesh=_sc_mesh(),
        scratch_types=[
            pltpu.VMEM((2, group, hk), I32),
            pltpu.VMEM((2, group, d), F32),
            pltpu.VMEM((group, hk), F32),
            pltpu.VMEM((GATHER_BUFFERS, GATHER_ROWS, tiles, LANES), F32),
            pltpu.SemaphoreType.DMA((GATHER_BUFFERS,)),
            pltpu.SemaphoreType.DMA((2,)),
        ],
        compiler_params=pltpu.CompilerParams(needs_layout_passes=False),
        name="peer_dots",
    )(xn, idx, table, after)


def _peer_mix(w, idx, table, t_sc):
    t, hk = w.shape
    tiles = table.shape[1]
    d = tiles * LANES
    per_worker, group, chunks = _sc_geometry(t_sc, hk)

    def body(w_hbm, idx_hbm, table_hbm, out_hbm, idx_v, w_v, o_v, ring, sems, stage_sems, acc_v):
        wid = lax.axis_index("c") * SC_SUBCORES + lax.axis_index("s")

        def compute(par, g, c, slot):
            wv = w_v[par, g, pl.ds(c * GATHER_ROWS, GATHER_ROWS)]
            ws = [jnp.full((SC_LANES,), wv[r], F32) for r in range(GATHER_ROWS)]

            @plsc.parallel_loop(0, tiles * MIX_BLOCKS_PER_TILE)
            def _(cb):
                tile, lane0 = cb // MIX_BLOCKS_PER_TILE, (cb % MIX_BLOCKS_PER_TILE) * MIX_BLOCK_LANES
                sums = []
                for u in range(MIX_LANE_STEPS):
                    part = [None] * 4
                    for r in range(GATHER_ROWS):
                        term = ws[r] * ring[slot, r, tile, pl.ds(lane0 + u * SC_LANES, SC_LANES)]
                        part[r % 4] = term if part[r % 4] is None else part[r % 4] + term
                    sums.append((part[0] + part[1]) + (part[2] + part[3]))
                for u in range(MIX_LANE_STEPS):
                    col = cb * MIX_BLOCK_LANES + u * SC_LANES
                    if c == 0:
                        acc_v[pl.ds(col, SC_LANES)] = sums[u]
                    elif c < chunks - 1:
                        acc_v[pl.ds(col, SC_LANES)] = acc_v[pl.ds(col, SC_LANES)] + sums[u]
                    else:
                        o_v[g, pl.ds(col, SC_LANES)] = acc_v[pl.ds(col, SC_LANES)] + sums[u]

        _token_stream(wid * per_worker, per_worker, group, chunks, idx_hbm, w_hbm, table_hbm, out_hbm,
                      idx_v, w_v, o_v, ring, sems, stage_sems, compute)

    return pl.kernel(
        body,
        out_type=jax.ShapeDtypeStruct((t, d), F32),
        mesh=_sc_mesh(),
        scratch_types=[
            pltpu.VMEM((2, group, hk), I32),
            pltpu.VMEM((2, group, hk), F32),
            pltpu.VMEM((group, d), F32),
            pltpu.VMEM((GATHER_BUFFERS, GATHER_ROWS, tiles, LANES), F32),
            pltpu.SemaphoreType.DMA((GATHER_BUFFERS,)),
            pltpu.SemaphoreType.DMA((2,)),
            pltpu.VMEM((d,), F32),
        ],
        compiler_params=pltpu.CompilerParams(needs_layout_passes=False),
        name="peer_mix",
    )(w, idx, table)


TC_PEER_TOKENS = 512
TC_PEER_STEP = 8


def _peer_tc_body(idx_ref, x_ref, gate_ref, p_all_ref, u_hbm, v_hbm, p_ref, ubuf, vbuf, sems, *, tb, hk):
    del p_all_ref

    def issue(i, slot):
        def one(j, carry):
            e = idx_ref[i, j]
            pltpu.make_async_copy(u_hbm.at[pl.ds(e, 1)], ubuf.at[slot, pl.ds(j, 1)], sems.at[0, slot]).start()
            pltpu.make_async_copy(v_hbm.at[pl.ds(e, 1)], vbuf.at[slot, pl.ds(j, 1)],
                                  sems.at[1, slot]).start(priority=1)
            return carry
        lax.fori_loop(0, hk, one, 0)

    def wait(slot):
        pltpu.make_async_copy(u_hbm.at[pl.ds(0, hk)], ubuf.at[slot], sems.at[0, slot]).wait()
        pltpu.make_async_copy(v_hbm.at[pl.ds(0, hk)], vbuf.at[slot], sems.at[1, slot]).wait()

    def as_row(col):
        return jnp.broadcast_to(col, (hk, hk)).T[0:1, :]

    def as_col(row):
        return jnp.broadcast_to(row, (hk, hk)).T[:, 0:1]

    sqrt_half = np.sqrt(0.5).astype(np.float32)
    issue(0, 0)
    for i in range(tb):
        slot = i % 2
        if i + 1 < tb:
            issue(i + 1, 1 - slot)
        wait(slot)
        x = x_ref[pl.ds(i, 1), :]
        dots = as_row(jnp.sum(ubuf[slot] * x, axis=1, keepdims=True))
        w = gate_ref[pl.ds(i, 1), :] * (0.5 * dots * (1.0 + lax.erf(dots * sqrt_half)))
        p_ref[pl.ds(i, 1), :] = jnp.sum(vbuf[slot] * as_col(w), axis=0, keepdims=True)


def _peer_tc(p_all, xn, idx, gate, u, v, tok0, tt):
    t, d = xn.shape
    hk = idx.shape[1]
    tb = TC_PEER_STEP
    assert tt % tb == 0 and tok0 % tb == 0 and hk == LANES
    first = tok0 // tb
    rows = lambda width, space=None: pl.BlockSpec((tb, width), lambda i: (first + i, 0), **(
        {} if space is None else {"memory_space": space}))
    return pl.pallas_call(
        functools.partial(_peer_tc_body, tb=tb, hk=hk),
        grid=(tt // tb,),
        in_specs=[rows(hk, pltpu.SMEM), rows(d), rows(hk), pl.BlockSpec(memory_space=pl.ANY),
                  pl.BlockSpec(memory_space=pl.ANY), pl.BlockSpec(memory_space=pl.ANY)],
        out_specs=rows(d),
        out_shape=jax.ShapeDtypeStruct(p_all.shape, F32),
        input_output_aliases={3: 0},
        scratch_shapes=[pltpu.VMEM((2, hk, d), F32), pltpu.VMEM((2, hk, d), F32),
                        pltpu.SemaphoreType.DMA((2, 2))],
        compiler_params=pltpu.CompilerParams(dimension_semantics=("arbitrary",)),
        name="peer_tc",
    )(idx, xn, gate, p_all, u, v)


def _gate_body(d_ref, gate_ref, w_ref):
    x = d_ref[...]
    sqrt_half = np.sqrt(0.5).astype(np.float32)
    w_ref[...] = gate_ref[...] * (0.5 * x * (1.0 + lax.erf(x * sqrt_half)))


def _gate(dots, gate):
    t, hk = dots.shape
    tm = min(t, LIGHT_ROW_TILE)
    assert t % tm == 0
    spec = pl.BlockSpec((tm, hk), lambda i: (i, 0))
    return pl.pallas_call(
        _gate_body, grid=(t // tm,), in_specs=[spec, spec], out_specs=spec,
        out_shape=jax.ShapeDtypeStruct((t, hk), F32),
        compiler_params=pltpu.CompilerParams(dimension_semantics=("parallel",)),
        name="gate",
    )(dots, gate)


def _residual_body(y_all_ref, h_ref, p_ref, g_ref, y_ref, *, final):
    del y_all_ref
    x = h_ref[...] + p_ref[...]
    y_ref[...] = _rms(x, g_ref[...]) if final else x


def _residual(y_all, y0, h2, h0, p, g, final):
    t, d = p.shape
    tm = min(t, LIGHT_ROW_TILE)
    assert t % tm == 0 and y0 % tm == 0 and h0 % tm == 0
    y_first, h_first = y0 // tm, h0 // tm
    return pl.pallas_call(
        functools.partial(_residual_body, final=final),
        grid=(t // tm,),
        in_specs=[pl.BlockSpec(memory_space=pl.ANY),
                  pl.BlockSpec((tm, d), lambda i: (h_first + i, 0)),
                  pl.BlockSpec((tm, d), lambda i: (i, 0)),
                  _const_spec((1, d))],
        out_specs=pl.BlockSpec((tm, d), lambda i: (y_first + i, 0)),
        out_shape=jax.ShapeDtypeStruct(y_all.shape, F32),
        input_output_aliases={0: 0},
        compiler_params=pltpu.CompilerParams(dimension_semantics=("parallel",)),
        name="residual",
    )(y_all, h2, p, g.reshape(1, d))


SEQ_PARTS = 8


def _num_parts(n, l):
    per_part = (n // SEQ_PARTS) * l
    ok = n % SEQ_PARTS == 0 and per_part % (SC_WORKERS * SUBLANES) == 0
    return SEQ_PARTS if ok else 1


MIX_LAG = 2
FIRST_PIECE_TOKENS = 1024


def _token_pieces(q, tokens):
    quantum = SC_WORKERS * GROUP_TOKENS
    rest = tokens - FIRST_PIECE_TOKENS
    if q == 0 and rest > 0 and FIRST_PIECE_TOKENS % quantum == 0 and rest % quantum == 0:
        return [(0, FIRST_PIECE_TOKENS), (FIRST_PIECE_TOKENS, rest)]
    return [(0, tokens)]


def _trunk(x, hist_a, hist_b, norm_mix_g, w_in, conv_a_w, conv_b_w, conv_b_b, conv_norm_g, conv_norm_b,
           out_norm_a_g, out_norm_b_g, w_out, norm_ffn_g, w_q, sub_keys, u_tiles, v_tiles, final_norm_g):
    n, l, d = x.shape
    depth = len(w_in)
    parts = _num_parts(n, l)
    seqs = n // parts
    new_a, new_b = [], []
    for layer in range(depth):
        last = layer == depth - 1
        y_all = jnp.zeros((n * l, d), F32)
        nas, nbs, mixed = [], [], []
        for q in range(parts):
            h, na, nb = _mixer(x, q * seqs, seqs, hist_a[layer], hist_b[layer], norm_mix_g[layer], w_in[layer],
                               conv_a_w[layer], conv_b_w[layer], conv_b_b[layer], conv_norm_g[layer],
                               conv_norm_b[layer], out_norm_a_g[layer], out_norm_b_g[layer], w_out[layer])
            h2 = h.reshape(seqs * l, d)
            for t0, tn in _token_pieces(q, seqs * l):
                xn, idx, gate = _route(h2, t0, tn, norm_ffn_g[layer], w_q[layer], sub_keys[layer])
                after = mixed[-MIX_LAG] if len(mixed) >= MIX_LAG else idx
                tc_n = TC_PEER_TOKENS if (q % 2 == 1 and tn == seqs * l and tn > 4 * TC_PEER_TOKENS) else 0
                (u_t, u_2d), (v_t, v_2d) = u_tiles[layer], v_tiles[layer]
                dots = _peer_dots(xn, idx, u_t, after, tn - tc_n)
                w = _gate(dots, gate)
                p = _peer_mix(w, idx, v_t, tn - tc_n)
                if tc_n:
                    p = _peer_tc(p, xn, idx, gate, u_2d, v_2d, tn - tc_n, tc_n)
                mixed.append(p)
                y_all = _residual(y_all, q * seqs * l + t0, h2, t0, p,
                                  final_norm_g if last else jnp.ones((d,), F32), final=last)
            nas.append(na)
            nbs.append(nb)
        x = y_all.reshape(n, l, d)
        new_a.append(jnp.concatenate(nas, axis=0))
        new_b.append(jnp.concatenate(nbs, axis=0))
    return x, jnp.stack(new_a), jnp.stack(new_b)


def kernel(x_prompt, x_sample, cache_conv_a, cache_conv_b, norm_mix_g, w_in, conv_a_w, conv_b_w, conv_b_b,
           conv_norm_g, conv_norm_b, out_norm_a_g, out_norm_b_g, w_out, norm_ffn_g, w_q, sub_keys,
           u_experts, v_experts, final_norm_g):
    depth = w_in.shape[0]
    _, heads, _, n_keys, d_key = sub_keys.shape
    u_tiles = [(_row_tiles(u_experts[layer]), u_experts[layer]) for layer in range(depth)]
    v_tiles = [(_row_tiles(v_experts[layer]), v_experts[layer]) for layer in range(depth)]
    w_in_s = [_split(w_in[layer]) for layer in range(depth)]
    w_out_s = [_split(w_out[layer]) for layer in range(depth)]
    w_q_s = [_split(w_q[layer]) for layer in range(depth)]
    sk_s = [_split(sub_keys[layer].reshape(heads * 2 * n_keys, d_key)).reshape(2, heads * 2, n_keys, d_key)
            for layer in range(depth)]
    weights = (norm_mix_g, w_in_s, conv_a_w, conv_b_w, conv_b_b, conv_norm_g, conv_norm_b, out_norm_a_g,
               out_norm_b_g, w_out_s, norm_ffn_g, w_q_s, sk_s, u_tiles, v_tiles, final_norm_g)
    n = x_prompt.shape[0]
    y_sample, conv_a_sample, conv_b_sample = _trunk(x_sample, cache_conv_a, cache_conv_b, *weights)
    zeros_a = jnp.zeros((depth, n) + cache_conv_a.shape[2:], x_prompt.dtype)
    zeros_b = jnp.zeros((depth, n) + cache_conv_b.shape[2:], x_prompt.dtype)
    y_prompt, conv_a_prompt, conv_b_prompt = _trunk(x_prompt, zeros_a, zeros_b, *weights)
    return (y_prompt, y_sample, conv_a_prompt, conv_b_prompt, conv_a_sample, conv_b_sample)
```

```python
import functools

import numpy as np
import jax
import jax.numpy as jnp
from jax import lax
from jax.experimental import pallas as pl
from jax.experimental.pallas import tpu as pltpu
from jax.experimental.pallas import tpu_sc as plsc

F32 = jnp.float32
I32 = jnp.int32
BF16 = jnp.bfloat16
EPS = 1e-6

LANES = 128
SUBLANES = 8
VMEM_LIMIT_BYTES = 56 * 1024 * 1024
ROW_TILE = 512
LIGHT_ROW_TILE = 1024
SPLIT_ROW_TILE = 256
SC_CORES = 2
SC_SUBCORES = 16
SC_LANES = 16
SC_WORKERS = SC_CORES * SC_SUBCORES

PEER_TOPK = 16


def _rms(x, g):
    return x * lax.rsqrt(jnp.mean(x * x, axis=-1, keepdims=True) + EPS) * g


def _const_spec(shape):
    zeros = (0,) * len(shape)
    return pl.BlockSpec(shape, lambda *_: zeros, pipeline_mode=pl.Buffered(1))


def _hi_lo(x):
    hi = x.astype(BF16)
    return hi, (x - hi.astype(F32)).astype(BF16)


def _dot3(a, b_hi, b_lo, dims=(((1,), (0,)), ((), ()))):
    a_hi, a_lo = _hi_lo(a)
    dot = functools.partial(lax.dot_general, dimension_numbers=dims, preferred_element_type=F32)
    return (dot(a_hi, b_lo) + dot(a_lo, b_hi)) + dot(a_hi, b_hi)


def _split_body(w_ref, s_ref):
    hi, lo = _hi_lo(w_ref[...])
    s_ref[0] = hi
    s_ref[1] = lo


def _split(w):
    r, c = w.shape
    tm = min(r, SPLIT_ROW_TILE)
    assert r % tm == 0
    return pl.pallas_call(
        _split_body, grid=(r // tm,),
        in_specs=[pl.BlockSpec((tm, c), lambda i: (i, 0))],
        out_specs=pl.BlockSpec((2, tm, c), lambda i: (0, i, 0)),
        out_shape=jax.ShapeDtypeStruct((2, r, c), BF16),
        compiler_params=pltpu.CompilerParams(dimension_semantics=("parallel",)),
        name="split",
    )(w)


CONV_ROW_BLOCK = 64


def _mixer_body(x_ref, ha_ref, hb_ref, gmix_ref, win_ref, caw_ref, cbw_ref, cbb_ref, cng_ref, cnb_ref,
                ga_ref, gb_ref, wout_ref, h_ref, na_ref, nb_ref, xa_s, xb_s, cb_s,
                *, tm, wa, wb, ka, kb, pad_a, pad_b):
    i = pl.program_id(1)

    @pl.when(i == 0)
    def _():
        xa_s[0:pad_a] = ha_ref[...]
        xb_s[0:pad_b] = hb_ref[...]

    x = x_ref[...]
    xn = _rms(x, gmix_ref[...])
    proj = _dot3(xn, win_ref[0], win_ref[1])
    h_a = proj[:, 0:wa]
    c_a = proj[:, wa:2 * wa]
    b_a = proj[:, 2 * wa:3 * wa]
    v_b = proj[:, 3 * wa:3 * wa + wb]
    g_b = proj[:, 3 * wa + wb:3 * wa + 2 * wb]

    xa_s[pad_a:pad_a + tm] = c_a * h_a
    conv_a = caw_ref[0:1, :] * xa_s[pad_a - (ka - 1):pad_a - (ka - 1) + tm]
    for k in range(1, ka):
        lo = pad_a - (ka - 1) + k
        conv_a = conv_a + caw_ref[k:k + 1, :] * xa_s[lo:lo + tm]
    y_a = _rms(b_a * conv_a, ga_ref[...])

    xb_s[pad_b:pad_b + tm] = v_b * jax.nn.sigmoid(g_b)
    rb = min(tm, CONV_ROW_BLOCK)
    for r0 in range(0, tm, rb):
        base = r0 + pad_b - (kb - 1)
        acc = cbw_ref[0:1, :] * xb_s[base:base + rb]
        for k in range(1, kb):
            acc = acc + cbw_ref[k:k + 1, :] * xb_s[base + k:base + k + rb]
        cb_s[r0:r0 + rb] = acc + cbb_ref[...]
    cb = cb_s[...]
    mu = jnp.mean(cb, axis=-1, keepdims=True)
    xc = cb - mu
    ln = xc * lax.rsqrt(jnp.mean(xc * xc, axis=-1, keepdims=True) + EPS) * cng_ref[...] + cnb_ref[...]
    z = ln * jax.nn.sigmoid(ln)
    y_b = _rms(z, gb_ref[...])

    y = (_dot3(y_a, wout_ref[0, 0:wa, :], wout_ref[1, 0:wa, :])
         + _dot3(y_b, wout_ref[0, wa:wa + wb, :], wout_ref[1, wa:wa + wb, :]))
    h_ref[...] = x + y

    xa_s[0:pad_a] = xa_s[tm:tm + pad_a]
    xb_s[0:pad_b] = xb_s[tm:tm + pad_b]

    @pl.when(i == pl.num_programs(1) - 1)
    def _():
        na_ref[...] = xa_s[0:pad_a]
        nb_ref[...] = xb_s[0:pad_b]


def _mixer(x, seq0, n, hist_a, hist_b, gmix, w_in, caw, cbw, cbb, cng, cnb, ga, gb, w_out):
    _, l, d = x.shape
    ka, wa = caw.shape
    kb, wb = cbw.shape
    pad_a = SUBLANES
    pad_b = -(-(kb - 1) // SUBLANES) * SUBLANES
    tm = min(l, ROW_TILE)
    assert l % tm == 0 and tm % min(tm, CONV_ROW_BLOCK) == 0
    assert tm >= pad_b and tm % SUBLANES == 0 and ka - 1 <= pad_a
    ha = jnp.pad(hist_a, ((0, 0), (pad_a - (ka - 1), 0), (0, 0)))
    hb = jnp.pad(hist_b, ((0, 0), (pad_b - (kb - 1), 0), (0, 0)))
    row = lambda v: v.reshape(1, -1)
    body = functools.partial(_mixer_body, tm=tm, wa=wa, wb=wb, ka=ka, kb=kb, pad_a=pad_a, pad_b=pad_b)
    h, na, nb = pl.pallas_call(
        body,
        grid=(n, l // tm),
        in_specs=[
            pl.BlockSpec((None, tm, d), lambda b, i: (seq0 + b, i, 0)),
            pl.BlockSpec((None, pad_a, wa), lambda b, i: (seq0 + b, 0, 0)),
            pl.BlockSpec((None, pad_b, wb), lambda b, i: (seq0 + b, 0, 0)),
            _const_spec((1, d)),
            _const_spec(w_in.shape),
            _const_spec(caw.shape),
            _const_spec(cbw.shape),
            _const_spec((1, wb)),
            _const_spec((1, wb)),
            _const_spec((1, wb)),
            _const_spec((1, wa)),
            _const_spec((1, wb)),
            _const_spec(w_out.shape),
        ],
        out_specs=[
            pl.BlockSpec((None, tm, d), lambda b, i: (b, i, 0)),
            pl.BlockSpec((None, pad_a, wa), lambda b, i: (b, 0, 0)),
            pl.BlockSpec((None, pad_b, wb), lambda b, i: (b, 0, 0)),
        ],
        out_shape=[
            jax.ShapeDtypeStruct((n, l, d), F32),
            jax.ShapeDtypeStruct((n, pad_a, wa), F32),
            jax.ShapeDtypeStruct((n, pad_b, wb), F32),
        ],
        scratch_shapes=[
            pltpu.VMEM((pad_a + tm, wa), F32),
            pltpu.VMEM((pad_b + tm, wb), F32),
            pltpu.VMEM((tm, wb), F32),
        ],
        compiler_params=pltpu.CompilerParams(
            dimension_semantics=("parallel", "arbitrary"), vmem_limit_bytes=VMEM_LIMIT_BYTES),
        name="mixer",
    )(x, ha, hb, row(gmix), w_in, caw, cbw, row(cbb), row(cng), row(cnb), row(ga), row(gb), w_out)
    return h, na[:, pad_a - (ka - 1):], nb[:, pad_b - (kb - 1):]


def _top_rows(s, k):
    n, lanes = s.shape
    rows = lax.broadcasted_iota(I32, s.shape, 0)
    slot = lax.broadcasted_iota(I32, (k, lanes), 0)
    vals = jnp.zeros((k, lanes), F32)
    ids = jnp.zeros((k, lanes), I32)
    for it in range(k):
        m = jnp.max(s, axis=0, keepdims=True)
        am = jnp.min(jnp.where(s == m, rows, n), axis=0, keepdims=True)
        vals = jnp.where(slot == it, m, vals)
        ids = jnp.where(slot == it, am, ids)
        s = jnp.where(rows == am, -jnp.inf, s)
    return vals, ids


def _combine(v1, i1, v2, i2, n_keys):
    k, lanes = v1.shape
    half = k // 2
    vals, pos, eid = [], [], []

    def add(a_lo, a_n, b_lo, b_n):
        vals.append(v1[a_lo:a_lo + a_n] + v2[b_lo:b_lo + b_n])
        eid.append(i1[a_lo:a_lo + a_n] * n_keys + i2[b_lo:b_lo + b_n])
        r = lax.broadcasted_iota(I32, (max(a_n, b_n), lanes), 0)
        pos.append((a_lo + r) * k + b_lo if a_n > 1 else a_lo * k + b_lo + r)

    add(0, 1, 0, k)
    for a in range(1, half):
        add(a, 1, 0, half)
    add(half, k - half, 0, 1)
    cand = jnp.concatenate(vals, axis=0)
    pos = jnp.concatenate(pos, axis=0)
    eid = jnp.concatenate(eid, axis=0)

    slot = lax.broadcasted_iota(I32, (k, lanes), 0)
    top_s = jnp.zeros((k, lanes), F32)
    top_e = jnp.zeros((k, lanes), I32)
    for it in range(k):
        m = jnp.max(cand, axis=0, keepdims=True)
        sel = jnp.min(jnp.where(cand == m, pos, k * k), axis=0, keepdims=True)
        hit = pos == sel
        e = jnp.max(jnp.where(hit, eid, -1), axis=0, keepdims=True)
        top_s = jnp.where(slot == it, m, top_s)
        top_e = jnp.where(slot == it, e, top_e)
        cand = jnp.where(hit, -jnp.inf, cand)
    return top_s, top_e


def _route_body(h_ref, g_ref, wq_ref, sk_ref, xn_ref, idx_ref, gate_ref, qh_s, ql_s, st_s, et_s, gt_s,
                *, tm, heads, n_keys, d_key):
    hn = _rms(h_ref[...], g_ref[...])
    xn_ref[...] = hn
    qh_s[...], ql_s[...] = _hi_lo(_dot3(hn, wq_ref[0], wq_ref[1]))
    dot_nt = functools.partial(lax.dot_general, dimension_numbers=(((1,), (1,)), ((), ())),
                               preferred_element_type=F32)
    for hp in range(2 * heads):
        cols = slice(hp * d_key, (hp + 1) * d_key)
        st_s[hp] = ((dot_nt(sk_ref[0, hp], ql_s[:, cols]) + dot_nt(sk_ref[1, hp], qh_s[:, cols]))
                    + dot_nt(sk_ref[0, hp], qh_s[:, cols]))
    n_chunk = tm // LANES

    def per_head_chunk(j, carry):
        hd = j // n_chunk
        col = pl.multiple_of((j % n_chunk) * LANES, LANES)
        v1, i1 = _top_rows(st_s[2 * hd, :, pl.ds(col, LANES)], PEER_TOPK)
        v2, i2 = _top_rows(st_s[2 * hd + 1, :, pl.ds(col, LANES)], PEER_TOPK)
        top_s, top_e = _combine(v1, i1, v2, i2, n_keys)
        p = jnp.exp(top_s - top_s[0:1])
        gate = p / jnp.sum(p, axis=0, keepdims=True)
        row = pl.multiple_of(hd * PEER_TOPK, PEER_TOPK)
        et_s[pl.ds(row, PEER_TOPK), pl.ds(col, LANES)] = top_e
        gt_s[pl.ds(row, PEER_TOPK), pl.ds(col, LANES)] = gate
        return carry

    lax.fori_loop(0, heads * n_chunk, per_head_chunk, 0)
    idx_ref[...] = et_s[...].T
    gate_ref[...] = gt_s[...].T


def _route(h2, tok0, t, g, w_q, sk):
    d = h2.shape[1]
    _, heads2, n_keys, d_key = sk.shape
    heads = heads2 // 2
    hk = heads * PEER_TOPK
    tm = min(t, ROW_TILE)
    assert t % tm == 0 and tok0 % tm == 0 and tm % LANES == 0 and d_key == LANES and n_keys % SUBLANES == 0
    first = tok0 // tm
    body = functools.partial(_route_body, tm=tm, heads=heads, n_keys=n_keys, d_key=d_key)
    return pl.pallas_call(
        body,
        grid=(t // tm,),
        in_specs=[
            pl.BlockSpec((tm, d), lambda i: (first + i, 0)),
            _const_spec((1, d)),
            _const_spec(w_q.shape),
            _const_spec(sk.shape),
        ],
        out_specs=[
            pl.BlockSpec((tm, d), lambda i: (i, 0)),
            pl.BlockSpec((tm, hk), lambda i: (i, 0)),
            pl.BlockSpec((tm, hk), lambda i: (i, 0)),
        ],
        out_shape=[
            jax.ShapeDtypeStruct((t, d), F32),
            jax.ShapeDtypeStruct((t, hk), I32),
            jax.ShapeDtypeStruct((t, hk), F32),
        ],
        scratch_shapes=[
            pltpu.VMEM((tm, 2 * heads * d_key), BF16),
            pltpu.VMEM((tm, 2 * heads * d_key), BF16),
            pltpu.VMEM((2 * heads, n_keys, tm), F32),
            pltpu.VMEM((hk, tm), I32),
            pltpu.VMEM((hk, tm), F32),
        ],
        compiler_params=pltpu.CompilerParams(
            dimension_semantics=("parallel",), vmem_limit_bytes=VMEM_LIMIT_BYTES),
        name="route",
    )(h2, g.reshape(1, d), w_q, sk)


GATHER_ROWS = 16
GATHER_BUFFERS = 4
LANE_STEPS = 4
BLOCK_LANES = LANE_STEPS * SC_LANES
BLOCKS_PER_TILE = LANES // BLOCK_LANES
MIX_LANE_STEPS = 2
MIX_BLOCK_LANES = MIX_LANE_STEPS * SC_LANES
MIX_BLOCKS_PER_TILE = LANES // MIX_BLOCK_LANES
GROUP_TOKENS = 16


def _sc_mesh():
    return plsc.VectorSubcoreMesh(core_axis_name="c", subcore_axis_name="s")


def _sc_geometry(t, hk):
    per_worker = t // SC_WORKERS
    group = min(per_worker, GROUP_TOKENS)
    chunks = hk // GATHER_ROWS
    assert t % SC_WORKERS == 0 and per_worker % group == 0 and group % SUBLANES == 0
    assert hk % GATHER_ROWS == 0 and chunks % GATHER_BUFFERS == 0 and GATHER_ROWS == SC_LANES
    return per_worker, group, chunks


def _token_stream(first_token, per_worker, group, chunks, idx_hbm, side_hbm, table_hbm, out_hbm,
                  idx_v, side_v, o_v, ring, sems, stage_sems, compute):
    n_groups = per_worker // group

    def stage(grp, par):
        tok0 = pl.multiple_of(first_token + grp * group, SUBLANES)
        return (pltpu.make_async_copy(idx_hbm.at[pl.ds(tok0, group)], idx_v.at[par], stage_sems.at[par]),
                pltpu.make_async_copy(side_hbm.at[pl.ds(tok0, group)], side_v.at[par], stage_sems.at[par]))

    def gather(par, g, c, slot):
        rows = idx_v[par, g, pl.ds(c * GATHER_ROWS, GATHER_ROWS)]
        return pltpu.make_async_copy(table_hbm.at[rows], ring.at[slot], sems.at[slot])

    for cp in stage(0, 0):
        cp.start()
    for cp in stage(0, 0):
        cp.wait()
    for c in range(GATHER_BUFFERS):
        gather(0, 0, c, c).start()

    @pl.loop(0, per_worker)
    def _(tk):
        grp = tk // group
        g = tk % group
        par = grp % 2
        more_groups = grp + 1 < n_groups

        @pl.when((g == 0) & more_groups)
        def _():
            for cp in stage(grp + 1, 1 - par):
                cp.start()

        @pl.when((g == group - 1) & more_groups)
        def _():
            for cp in stage(grp + 1, 1 - par):
                cp.wait()

        for c in range(chunks):
            slot = c % GATHER_BUFFERS
            gather(par, g, c, slot).wait()
            compute(par, g, c, slot)
            nxt = c + GATHER_BUFFERS
            if nxt < chunks:
                gather(par, g, nxt, slot).start()
            else:
                @pl.when(tk + 1 < per_worker)
                def _():
                    gather(((tk + 1) // group) % 2, (tk + 1) % group, nxt - chunks, slot).start()

        @pl.when(g == group - 1)
        def _():
            tok0 = pl.multiple_of(first_token + grp * group, SUBLANES)
            pltpu.sync_copy(o_v, out_hbm.at[pl.ds(tok0, group)])


def _row_tiles(table):
    e, d = table.shape
    assert d % LANES == 0
    return table.reshape(e, d // LANES, LANES)


def _peer_dots(xn, idx, table, after):
    t, d = xn.shape
    hk = idx.shape[1]
    per_worker, group, chunks = _sc_geometry(t, hk)
    tiles = table.shape[1]

    def body(xn_hbm, idx_hbm, table_hbm, after_hbm, out_hbm, idx_v, x_v, o_v, ring, sems, stage_sems):
        del after_hbm
        wid = lax.axis_index("c") * SC_SUBCORES + lax.axis_index("s")
        lane = lax.iota(I32, SC_LANES)

        def compute(par, g, c, slot):
            def col_block(cb, accs):
                tile, lane0 = cb // BLOCKS_PER_TILE, (cb % BLOCKS_PER_TILE) * BLOCK_LANES
                for u in range(LANE_STEPS):
                    xc = x_v[par, g, pl.ds(cb * BLOCK_LANES + u * SC_LANES, SC_LANES)]
                    accs = tuple(a + ring[slot, r, tile, pl.ds(lane0 + u * SC_LANES, SC_LANES)] * xc
                                 for r, a in enumerate(accs))
                return accs

            zero = jnp.zeros((SC_LANES,), F32)
            accs = lax.fori_loop(0, tiles * BLOCKS_PER_TILE, col_block, (zero,) * GATHER_ROWS)
            tot = zero
            for r in range(GATHER_ROWS):
                tot = jnp.where(lane == r, jnp.sum(accs[r]), tot)
            o_v[g, pl.ds(c * GATHER_ROWS, GATHER_ROWS)] = tot

        _token_stream(wid * per_worker, per_worker, group, chunks, idx_hbm, xn_hbm, table_hbm, out_hbm,
                      idx_v, x_v, o_v, ring, sems, stage_sems, compute)

    return pl.kernel(
        body,
        out_type=jax.ShapeDtypeStruct((t, hk), F32),
        mesh=_sc_mesh(),
        scratch_types=[
            pltpu.VMEM((2, group, hk), I32),
            pltpu.VMEM((2, group, d), F32),
            pltpu.VMEM((group, hk), F32),
            pltpu.VMEM((GATHER_BUFFERS, GATHER_ROWS, tiles, LANES), F32),
            pltpu.SemaphoreType.DMA((GATHER_BUFFERS,)),
            pltpu.SemaphoreType.DMA((2,)),
        ],
        compiler_params=pltpu.CompilerParams(needs_layout_passes=False),
        name="peer_dots",
    )(xn, idx, table, after)


def _peer_mix(w, idx, table):
    t, hk = w.shape
    tiles = table.shape[1]
    d = tiles * LANES
    per_worker, group, chunks = _sc_geometry(t, hk)

    def body(w_hbm, idx_hbm, table_hbm, out_hbm, idx_v, w_v, o_v, ring, sems, stage_sems, acc_v):
        wid = lax.axis_index("c") * SC_SUBCORES + lax.axis_index("s")

        def compute(par, g, c, slot):
            wv = w_v[par, g, pl.ds(c * GATHER_ROWS, GATHER_ROWS)]
            ws = [jnp.full((SC_LANES,), wv[r], F32) for r in range(GATHER_ROWS)]

            @plsc.parallel_loop(0, tiles * MIX_BLOCKS_PER_TILE)
            def _(cb):
                tile, lane0 = cb // MIX_BLOCKS_PER_TILE, (cb % MIX_BLOCKS_PER_TILE) * MIX_BLOCK_LANES
                sums = []
                for u in range(MIX_LANE_STEPS):
                    part = [None] * 4
                    for r in range(GATHER_ROWS):
                        term = ws[r] * ring[slot, r, tile, pl.ds(lane0 + u * SC_LANES, SC_LANES)]
                        part[r % 4] = term if part[r % 4] is None else part[r % 4] + term
                    sums.append((part[0] + part[1]) + (part[2] + part[3]))
                for u in range(MIX_LANE_STEPS):
                    col = cb * MIX_BLOCK_LANES + u * SC_LANES
                    if c == 0:
                        acc_v[pl.ds(col, SC_LANES)] = sums[u]
                    elif c < chunks - 1:
                        acc_v[pl.ds(col, SC_LANES)] = acc_v[pl.ds(col, SC_LANES)] + sums[u]
                    else:
                        o_v[g, pl.ds(col, SC_LANES)] = acc_v[pl.ds(col, SC_LANES)] + sums[u]

        _token_stream(wid * per_worker, per_worker, group, chunks, idx_hbm, w_hbm, table_hbm, out_hbm,
                      idx_v, w_v, o_v, ring, sems, stage_sems, compute)

    return pl.kernel(
        body,
        out_type=jax.ShapeDtypeStruct((t, d), F32),
        mesh=_sc_mesh(),
        scratch_types=[
            pltpu.VMEM((2, group, hk), I32),
            pltpu.VMEM((2, group, hk), F32),
            pltpu.VMEM((group, d), F32),
            pltpu.VMEM((GATHER_BUFFERS, GATHER_ROWS, tiles, LANES), F32),
            pltpu.SemaphoreType.DMA((GATHER_BUFFERS,)),
            pltpu.SemaphoreType.DMA((2,)),
            pltpu.VMEM((d,), F32),
        ],
        compiler_params=pltpu.CompilerParams(needs_layout_passes=False),
        name="peer_mix",
    )(w, idx, table)


def _gate_body(d_ref, gate_ref, w_ref):
    x = d_ref[...]
    sqrt_half = np.sqrt(0.5).astype(np.float32)
    w_ref[...] = gate_ref[...] * (0.5 * x * (1.0 + lax.erf(x * sqrt_half)))


def _gate(dots, gate):
    t, hk = dots.shape
    tm = min(t, LIGHT_ROW_TILE)
    assert t % tm == 0
    spec = pl.BlockSpec((tm, hk), lambda i: (i, 0))
    return pl.pallas_call(
        _gate_body, grid=(t // tm,), in_specs=[spec, spec], out_specs=spec,
        out_shape=jax.ShapeDtypeStruct((t, hk), F32),
        compiler_params=pltpu.CompilerParams(dimension_semantics=("parallel",)),
        name="gate",
    )(dots, gate)


def _residual_body(*refs, final):
    h_ref, p_ref, g_ref, y_ref = refs[-4:]
    x = h_ref[...] + p_ref[...]
    y_ref[...] = _rms(x, g_ref[...]) if final else x


def _residual(y_all, rows, y0, h2, h0, p, g, final):
    t, d = p.shape
    tm = min(t, LIGHT_ROW_TILE)
    assert t % tm == 0 and y0 % tm == 0 and h0 % tm == 0
    y_first, h_first = y0 // tm, h0 // tm
    first = y_all is None
    return pl.pallas_call(
        functools.partial(_residual_body, final=final),
        grid=(t // tm,),
        in_specs=([] if first else [pl.BlockSpec(memory_space=pl.ANY)]) + [
            pl.BlockSpec((tm, d), lambda i: (h_first + i, 0)),
            pl.BlockSpec((tm, d), lambda i: (i, 0)),
            _const_spec((1, d))],
        out_specs=pl.BlockSpec((tm, d), lambda i: (y_first + i, 0)),
        out_shape=jax.ShapeDtypeStruct((rows, d), F32),
        input_output_aliases={} if first else {0: 0},
        compiler_params=pltpu.CompilerParams(dimension_semantics=("parallel",)),
        name="residual",
    )(*(() if first else (y_all,)), h2, p, g.reshape(1, d))


SEQ_PARTS = 8


def _num_parts(n, l):
    per_part = (n // SEQ_PARTS) * l
    ok = n % SEQ_PARTS == 0 and per_part % (SC_WORKERS * SUBLANES) == 0
    return SEQ_PARTS if ok else 1


MIX_LAG = 2
FIRST_PIECE_TOKENS = 1024


def _token_pieces(q, tokens):
    quantum = SC_WORKERS * GROUP_TOKENS
    rest = tokens - FIRST_PIECE_TOKENS
    if q == 0 and rest > 0 and FIRST_PIECE_TOKENS % quantum == 0 and rest % quantum == 0:
        return [(0, FIRST_PIECE_TOKENS), (FIRST_PIECE_TOKENS, rest)]
    return [(0, tokens)]


def _trunk(x, hist_a, hist_b, norm_mix_g, w_in, conv_a_w, conv_b_w, conv_b_b, conv_norm_g, conv_norm_b,
           out_norm_a_g, out_norm_b_g, w_out, norm_ffn_g, w_q, sub_keys, u_tiles, v_tiles, final_norm_g):
    n, l, d = x.shape
    depth = len(w_in)
    parts = _num_parts(n, l)
    seqs = n // parts
    new_a, new_b = [], []
    for layer in range(depth):
        last = layer == depth - 1
        y_all = None
        nas, nbs, mixed = [], [], []
        for q in range(parts):
            h, na, nb = _mixer(x, q * seqs, seqs, hist_a[layer], hist_b[layer], norm_mix_g[layer], w_in[layer],
                               conv_a_w[layer], conv_b_w[layer], conv_b_b[layer], conv_norm_g[layer],
                               conv_norm_b[layer], out_norm_a_g[layer], out_norm_b_g[layer], w_out[layer])
            h2 = h.reshape(seqs * l, d)
            for t0, tn in _token_pieces(q, seqs * l):
                xn, idx, gate = _route(h2, t0, tn, norm_ffn_g[layer], w_q[layer], sub_keys[layer])
                after = mixed[-MIX_LAG] if len(mixed) >= MIX_LAG else idx
                dots = _peer_dots(xn, idx, u_tiles[layer], after)
                w = _gate(dots, gate)
                p = _peer_mix(w, idx, v_tiles[layer])
                mixed.append(p)
                y_all = _residual(y_all, n * l, q * seqs * l + t0, h2, t0, p,
                                  final_norm_g if last else jnp.ones((d,), F32), final=last)
            nas.append(na)
            nbs.append(nb)
        x = y_all.reshape(n, l, d)
        new_a.append(jnp.concatenate(nas, axis=0))
        new_b.append(jnp.concatenate(nbs, axis=0))
    return x, jnp.stack(new_a), jnp.stack(new_b)


def kernel(x_prompt, x_sample, cache_conv_a, cache_conv_b, norm_mix_g, w_in, conv_a_w, conv_b_w, conv_b_b,
           conv_norm_g, conv_norm_b, out_norm_a_g, out_norm_b_g, w_out, norm_ffn_g, w_q, sub_keys,
           u_experts, v_experts, final_norm_g):
    depth = w_in.shape[0]
    _, heads, _, n_keys, d_key = sub_keys.shape
    u_tiles = [_row_tiles(u_experts[layer]) for layer in range(depth)]
    v_tiles = [_row_tiles(v_experts[layer]) for layer in range(depth)]
    w_in_s = [_split(w_in[layer]) for layer in range(depth)]
    w_out_s = [_split(w_out[layer]) for layer in range(depth)]
    w_q_s = [_split(w_q[layer]) for layer in range(depth)]
    sk_s = [_split(sub_keys[layer].reshape(heads * 2 * n_keys, d_key)).reshape(2, heads * 2, n_keys, d_key)
            for layer in range(depth)]
    weights = (norm_mix_g, w_in_s, conv_a_w, conv_b_w, conv_b_b, conv_norm_g, conv_norm_b, out_norm_a_g,
               out_norm_b_g, w_out_s, norm_ffn_g, w_q_s, sk_s, u_tiles, v_tiles, final_norm_g)
    n = x_prompt.shape[0]
    y_sample, conv_a_sample, conv_b_sample = _trunk(x_sample, cache_conv_a, cache_conv_b, *weights)
    zeros_a = jnp.zeros((depth, n) + cache_conv_a.shape[2:], x_prompt.dtype)
    zeros_b = jnp.zeros((depth, n) + cache_conv_b.shape[2:], x_prompt.dtype)
    y_prompt, conv_a_prompt, conv_b_prompt = _trunk(x_prompt, zeros_a, zeros_b, *weights)
    return (y_prompt, y_sample, conv_a_prompt, conv_b_prompt, conv_a_sample, conv_b_sample)
```

```python
import functools

import numpy as np
import jax
import jax.numpy as jnp
from jax import lax
from jax.experimental import pallas as pl
from jax.experimental.pallas import tpu as pltpu
from jax.experimental.pallas import tpu_sc as plsc

F32 = jnp.float32
I32 = jnp.int32
BF16 = jnp.bfloat16
EPS = 1e-6

LANES = 128
SUBLANES = 8
VMEM_LIMIT_BYTES = 56 * 1024 * 1024
ROW_TILE = 512
LIGHT_ROW_TILE = 1024
SPLIT_ROW_TILE = 256
SC_CORES = 2
SC_SUBCORES = 16
SC_LANES = 16
SC_WORKERS = SC_CORES * SC_SUBCORES

PEER_TOPK = 16


def _rms(x, g):
    return x * lax.rsqrt(jnp.mean(x * x, axis=-1, keepdims=True) + EPS) * g


def _const_spec(shape):
    zeros = (0,) * len(shape)
    return pl.BlockSpec(shape, lambda *_: zeros, pipeline_mode=pl.Buffered(1))


def _hi_lo(x):
    hi = x.astype(BF16)
    return hi, (x - hi.astype(F32)).astype(BF16)


def _dot3(a, b_hi, b_lo, dims=(((1,), (0,)), ((), ()))):
    a_hi, a_lo = _hi_lo(a)
    dot = functools.partial(lax.dot_general, dimension_numbers=dims, preferred_element_type=F32)
    return (dot(a_hi, b_lo) + dot(a_lo, b_hi)) + dot(a_hi, b_hi)


def _split_body(w_ref, s_ref):
    hi, lo = _hi_lo(w_ref[...])
    s_ref[0] = hi
    s_ref[1] = lo


def _split(w):
    r, c = w.shape
    tm = min(r, SPLIT_ROW_TILE)
    assert r % tm == 0
    return pl.pallas_call(
        _split_body, grid=(r // tm,),
        in_specs=[pl.BlockSpec((tm, c), lambda i: (i, 0))],
        out_specs=pl.BlockSpec((2, tm, c), lambda i: (0, i, 0)),
        out_shape=jax.ShapeDtypeStruct((2, r, c), BF16),
        compiler_params=pltpu.CompilerParams(dimension_semantics=("parallel",)),
        name="split",
    )(w)


CONV_ROW_BLOCK = 64


def _mixer_body(x_ref, ha_ref, hb_ref, gmix_ref, win_ref, caw_ref, cbw_ref, cbb_ref, cng_ref, cnb_ref,
                ga_ref, gb_ref, wout_ref, h_ref, na_ref, nb_ref, xa_s, xb_s, cb_s,
                *, tm, wa, wb, ka, kb, pad_a, pad_b):
    i = pl.program_id(1)

    @pl.when(i == 0)
    def _():
        xa_s[0:pad_a] = ha_ref[...]
        xb_s[0:pad_b] = hb_ref[...]

    x = x_ref[...]
    xn = _rms(x, gmix_ref[...])
    proj = _dot3(xn, win_ref[0], win_ref[1])
    h_a = proj[:, 0:wa]
    c_a = proj[:, wa:2 * wa]
    b_a = proj[:, 2 * wa:3 * wa]
    v_b = proj[:, 3 * wa:3 * wa + wb]
    g_b = proj[:, 3 * wa + wb:3 * wa + 2 * wb]

    xa_s[pad_a:pad_a + tm] = c_a * h_a
    conv_a = caw_ref[0:1, :] * xa_s[pad_a - (ka - 1):pad_a - (ka - 1) + tm]
    for k in range(1, ka):
        lo = pad_a - (ka - 1) + k
        conv_a = conv_a + caw_ref[k:k + 1, :] * xa_s[lo:lo + tm]
    y_a = _rms(b_a * conv_a, ga_ref[...])

    xb_s[pad_b:pad_b + tm] = v_b * jax.nn.sigmoid(g_b)
    rb = min(tm, CONV_ROW_BLOCK)
    for r0 in range(0, tm, rb):
        base = r0 + pad_b - (kb - 1)
        acc = cbw_ref[0:1, :] * xb_s[base:base + rb]
        for k in range(1, kb):
            acc = acc + cbw_ref[k:k + 1, :] * xb_s[base + k:base + k + rb]
        cb_s[r0:r0 + rb] = acc + cbb_ref[...]
    cb = cb_s[...]
    mu = jnp.mean(cb, axis=-1, keepdims=True)
    xc = cb - mu
    ln = xc * lax.rsqrt(jnp.mean(xc * xc, axis=-1, keepdims=True) + EPS) * cng_ref[...] + cnb_ref[...]
    z = ln * jax.nn.sigmoid(ln)
    y_b = _rms(z, gb_ref[...])

    y = (_dot3(y_a, wout_ref[0, 0:wa, :], wout_ref[1, 0:wa, :])
         + _dot3(y_b, wout_ref[0, wa:wa + wb, :], wout_ref[1, wa:wa + wb, :]))
    h_ref[...] = x + y

    xa_s[0:pad_a] = xa_s[tm:tm + pad_a]
    xb_s[0:pad_b] = xb_s[tm:tm + pad_b]

    @pl.when(i == pl.num_programs(1) - 1)
    def _():
        na_ref[...] = xa_s[0:pad_a]
        nb_ref[...] = xb_s[0:pad_b]


def _mixer(x, seq0, n, hist_a, hist_b, gmix, w_in, caw, cbw, cbb, cng, cnb, ga, gb, w_out):
    _, l, d = x.shape
    ka, wa = caw.shape
    kb, wb = cbw.shape
    pad_a = SUBLANES
    pad_b = -(-(kb - 1) // SUBLANES) * SUBLANES
    tm = min(l, ROW_TILE)
    assert l % tm == 0 and tm % min(tm, CONV_ROW_BLOCK) == 0
    assert tm >= pad_b and tm % SUBLANES == 0 and ka - 1 <= pad_a
    ha = jnp.pad(hist_a, ((0, 0), (pad_a - (ka - 1), 0), (0, 0)))
    hb = jnp.pad(hist_b, ((0, 0), (pad_b - (kb - 1), 0), (0, 0)))
    row = lambda v: v.reshape(1, -1)
    body = functools.partial(_mixer_body, tm=tm, wa=wa, wb=wb, ka=ka, kb=kb, pad_a=pad_a, pad_b=pad_b)
    h, na, nb = pl.pallas_call(
        body,
        grid=(n, l // tm),
        in_specs=[
            pl.BlockSpec((None, tm, d), lambda b, i: (seq0 + b, i, 0)),
            pl.BlockSpec((None, pad_a, wa), lambda b, i: (seq0 + b, 0, 0)),
            pl.BlockSpec((None, pad_b, wb), lambda b, i: (seq0 + b, 0, 0)),
            _const_spec((1, d)),
            _const_spec(w_in.shape),
            _const_spec(caw.shape),
            _const_spec(cbw.shape),
            _const_spec((1, wb)),
            _const_spec((1, wb)),
            _const_spec((1, wb)),
            _const_spec((1, wa)),
            _const_spec((1, wb)),
            _const_spec(w_out.shape),
        ],
        out_specs=[
            pl.BlockSpec((None, tm, d), lambda b, i: (b, i, 0)),
            pl.BlockSpec((None, pad_a, wa), lambda b, i: (b, 0, 0)),
            pl.BlockSpec((None, pad_b, wb), lambda b, i: (b, 0, 0)),
        ],
        out_shape=[
            jax.ShapeDtypeStruct((n, l, d), F32),
            jax.ShapeDtypeStruct((n, pad_a, wa), F32),
            jax.ShapeDtypeStruct((n, pad_b, wb), F32),
        ],
        scratch_shapes=[
            pltpu.VMEM((pad_a + tm, wa), F32),
            pltpu.VMEM((pad_b + tm, wb), F32),
            pltpu.VMEM((tm, wb), F32),
        ],
        compiler_params=pltpu.CompilerParams(
            dimension_semantics=("parallel", "arbitrary"), vmem_limit_bytes=VMEM_LIMIT_BYTES),
        name="mixer",
    )(x, ha, hb, row(gmix), w_in, caw, cbw, row(cbb), row(cng), row(cnb), row(ga), row(gb), w_out)
    return h, na[:, pad_a - (ka - 1):], nb[:, pad_b - (kb - 1):]


def _top_rows(s, k):
    n, lanes = s.shape
    rows = lax.broadcasted_iota(I32, s.shape, 0)
    slot = lax.broadcasted_iota(I32, (k, lanes), 0)
    vals = jnp.zeros((k, lanes), F32)
    ids = jnp.zeros((k, lanes), I32)
    for it in range(k):
        m = jnp.max(s, axis=0, keepdims=True)
        am = jnp.min(jnp.where(s == m, rows, n), axis=0, keepdims=True)
        vals = jnp.where(slot == it, m, vals)
        ids = jnp.where(slot == it, am, ids)
        s = jnp.where(rows == am, -jnp.inf, s)
    return vals, ids


def _combine(v1, i1, v2, i2, n_keys):
    k, lanes = v1.shape
    half = k // 2
    vals, pos, eid = [], [], []

    def add(a_lo, a_n, b_lo, b_n):
        vals.append(v1[a_lo:a_lo + a_n] + v2[b_lo:b_lo + b_n])
        eid.append(i1[a_lo:a_lo + a_n] * n_keys + i2[b_lo:b_lo + b_n])
        r = lax.broadcasted_iota(I32, (max(a_n, b_n), lanes), 0)
        pos.append((a_lo + r) * k + b_lo if a_n > 1 else a_lo * k + b_lo + r)

    add(0, 1, 0, k)
    for a in range(1, half):
        add(a, 1, 0, half)
    add(half, k - half, 0, 1)
    cand = jnp.concatenate(vals, axis=0)
    pos = jnp.concatenate(pos, axis=0)
    eid = jnp.concatenate(eid, axis=0)

    slot = lax.broadcasted_iota(I32, (k, lanes), 0)
    top_s = jnp.zeros((k, lanes), F32)
    top_e = jnp.zeros((k, lanes), I32)
    for it in range(k):
        m = jnp.max(cand, axis=0, keepdims=True)
        sel = jnp.min(jnp.where(cand == m, pos, k * k), axis=0, keepdims=True)
        hit = pos == sel
        e = jnp.max(jnp.where(hit, eid, -1), axis=0, keepdims=True)
        top_s = jnp.where(slot == it, m, top_s)
        top_e = jnp.where(slot == it, e, top_e)
        cand = jnp.where(hit, -jnp.inf, cand)
    return top_s, top_e


def _route_body(h_ref, g_ref, wq_ref, sk_ref, xn_ref, idx_ref, gate_ref, qh_s, ql_s, st_s, et_s, gt_s,
                *, tm, heads, n_keys, d_key):
    hn = _rms(h_ref[...], g_ref[...])
    xn_ref[...] = hn
    qh_s[...], ql_s[...] = _hi_lo(_dot3(hn, wq_ref[0], wq_ref[1]))
    dot_nt = functools.partial(lax.dot_general, dimension_numbers=(((1,), (1,)), ((), ())),
                               preferred_element_type=F32)
    for hp in range(2 * heads):
        cols = slice(hp * d_key, (hp + 1) * d_key)
        st_s[hp] = ((dot_nt(sk_ref[0, hp], ql_s[:, cols]) + dot_nt(sk_ref[1, hp], qh_s[:, cols]))
                    + dot_nt(sk_ref[0, hp], qh_s[:, cols]))
    n_chunk = tm // LANES

    def per_head_chunk(j, carry):
        hd = j // n_chunk
        col = pl.multiple_of((j % n_chunk) * LANES, LANES)
        v1, i1 = _top_rows(st_s[2 * hd, :, pl.ds(col, LANES)], PEER_TOPK)
        v2, i2 = _top_rows(st_s[2 * hd + 1, :, pl.ds(col, LANES)], PEER_TOPK)
        top_s, top_e = _combine(v1, i1, v2, i2, n_keys)
        p = jnp.exp(top_s - top_s[0:1])
        gate = p / jnp.sum(p, axis=0, keepdims=True)
        row = pl.multiple_of(hd * PEER_TOPK, PEER_TOPK)
        et_s[pl.ds(row, PEER_TOPK), pl.ds(col, LANES)] = top_e
        gt_s[pl.ds(row, PEER_TOPK), pl.ds(col, LANES)] = gate
        return carry

    lax.fori_loop(0, heads * n_chunk, per_head_chunk, 0)
    idx_ref[...] = et_s[...].T
    gate_ref[...] = gt_s[...].T


def _route(h2, tok0, t, g, w_q, sk):
    d = h2.shape[1]
    _, heads2, n_keys, d_key = sk.shape
    heads = heads2 // 2
    hk = heads * PEER_TOPK
    tm = min(t, ROW_TILE)
    assert t % tm == 0 and tok0 % tm == 0 and tm % LANES == 0 and d_key == LANES and n_keys % SUBLANES == 0
    first = tok0 // tm
    body = functools.partial(_route_body, tm=tm, heads=heads, n_keys=n_keys, d_key=d_key)
    return pl.pallas_call(
        body,
        grid=(t // tm,),
        in_specs=[
            pl.BlockSpec((tm, d), lambda i: (first + i, 0)),
            _const_spec((1, d)),
            _const_spec(w_q.shape),
            _const_spec(sk.shape),
        ],
        out_specs=[
            pl.BlockSpec((tm, d), lambda i: (i, 0)),
            pl.BlockSpec((tm, hk), lambda i: (i, 0)),
            pl.BlockSpec((tm, hk), lambda i: (i, 0)),
        ],
        out_shape=[
            jax.ShapeDtypeStruct((t, d), F32),
            jax.ShapeDtypeStruct((t, hk), I32),
            jax.ShapeDtypeStruct((t, hk), F32),
        ],
        scratch_shapes=[
            pltpu.VMEM((tm, 2 * heads * d_key), BF16),
            pltpu.VMEM((tm, 2 * heads * d_key), BF16),
            pltpu.VMEM((2 * heads, n_keys, tm), F32),
            pltpu.VMEM((hk, tm), I32),
            pltpu.VMEM((hk, tm), F32),
        ],
        compiler_params=pltpu.CompilerParams(
            dimension_semantics=("parallel",), vmem_limit_bytes=VMEM_LIMIT_BYTES),
        name="route",
    )(h2, g.reshape(1, d), w_q, sk)


GATHER_ROWS = 16
GATHER_BUFFERS = 4
LANE_STEPS = 4
BLOCK_LANES = LANE_STEPS * SC_LANES
BLOCKS_PER_TILE = LANES // BLOCK_LANES
MIX_LANE_STEPS = 2
MIX_BLOCK_LANES = MIX_LANE_STEPS * SC_LANES
MIX_BLOCKS_PER_TILE = LANES // MIX_BLOCK_LANES
GROUP_TOKENS = 16


def _sc_mesh():
    return plsc.VectorSubcoreMesh(core_axis_name="c", subcore_axis_name="s")


def _sc_geometry(t, hk):
    per_worker = t // SC_WORKERS
    group = min(per_worker, GROUP_TOKENS)
    chunks = hk // GATHER_ROWS
    assert t % SC_WORKERS == 0 and per_worker % group == 0 and group % SUBLANES == 0
    assert hk % GATHER_ROWS == 0 and chunks % GATHER_BUFFERS == 0 and GATHER_ROWS == SC_LANES
    return per_worker, group, chunks


def _token_stream(first_token, per_worker, group, chunks, idx_hbm, side_hbm, table_hbm, out_hbm,
                  idx_v, side_v, o_v, ring, sems, stage_sems, compute):
    n_groups = per_worker // group

    def stage(grp, par):
        tok0 = pl.multiple_of(first_token + grp * group, SUBLANES)
        return (pltpu.make_async_copy(idx_hbm.at[pl.ds(tok0, group)], idx_v.at[par], stage_sems.at[par]),
                pltpu.make_async_copy(side_hbm.at[pl.ds(tok0, group)], side_v.at[par], stage_sems.at[par]))

    def gather(par, g, c, slot):
        rows = idx_v[par, g, pl.ds(c * GATHER_ROWS, GATHER_ROWS)]
        return pltpu.make_async_copy(table_hbm.at[rows], ring.at[slot], sems.at[slot])

    for cp in stage(0, 0):
        cp.start()
    for cp in stage(0, 0):
        cp.wait()
    for c in range(GATHER_BUFFERS):
        gather(0, 0, c, c).start()

    @pl.loop(0, per_worker)
    def _(tk):
        grp = tk // group
        g = tk % group
        par = grp % 2
        more_groups = grp + 1 < n_groups

        @pl.when((g == 0) & more_groups)
        def _():
            for cp in stage(grp + 1, 1 - par):
                cp.start()

        @pl.when((g == group - 1) & more_groups)
        def _():
            for cp in stage(grp + 1, 1 - par):
                cp.wait()

        for c in range(chunks):
            slot = c % GATHER_BUFFERS
            gather(par, g, c, slot).wait()
            compute(par, g, c, slot)
            nxt = c + GATHER_BUFFERS
            if nxt < chunks:
                gather(par, g, nxt, slot).start()
            else:
                @pl.when(tk + 1 < per_worker)
                def _():
                    gather(((tk + 1) // group) % 2, (tk + 1) % group, nxt - chunks, slot).start()

        @pl.when(g == group - 1)
        def _():
            tok0 = pl.multiple_of(first_token + grp * group, SUBLANES)
            pltpu.sync_copy(o_v, out_hbm.at[pl.ds(tok0, group)])


def _row_tiles(table):
    e, d = table.shape
    assert d % LANES == 0
    return table.reshape(e, d // LANES, LANES)


def _peer_dots(xn, idx, table, after):
    t, d = xn.shape
    hk = idx.shape[1]
    per_worker, group, chunks = _sc_geometry(t, hk)
    tiles = table.shape[1]

    def body(xn_hbm, idx_hbm, table_hbm, after_hbm, out_hbm, idx_v, x_v, o_v, ring, sems, stage_sems):
        del after_hbm
        wid = lax.axis_index("c") * SC_SUBCORES + lax.axis_index("s")
        lane = lax.iota(I32, SC_LANES)

        def compute(par, g, c, slot):
            def col_block(cb, accs):
                tile, lane0 = cb // BLOCKS_PER_TILE, (cb % BLOCKS_PER_TILE) * BLOCK_LANES
                for u in range(LANE_STEPS):
                    xc = x_v[par, g, pl.ds(cb * BLOCK_LANES + u * SC_LANES, SC_LANES)]
                    accs = tuple(a + ring[slot, r, tile, pl.ds(lane0 + u * SC_LANES, SC_LANES)] * xc
                                 for r, a in enumerate(accs))
                return accs

            zero = jnp.zeros((SC_LANES,), F32)
            accs = lax.fori_loop(0, tiles * BLOCKS_PER_TILE, col_block, (zero,) * GATHER_ROWS)
            tot = zero
            for r in range(GATHER_ROWS):
                tot = jnp.where(lane == r, jnp.sum(accs[r]), tot)
            o_v[g, pl.ds(c * GATHER_ROWS, GATHER_ROWS)] = tot

        _token_stream(wid * per_worker, per_worker, group, chunks, idx_hbm, xn_hbm, table_hbm, out_hbm,
                      idx_v, x_v, o_v, ring, sems, stage_sems, compute)

    return pl.kernel(
        body,
        out_type=jax.ShapeDtypeStruct((t, hk), F32),
        mesh=_sc_mesh(),
        scratch_types=[
            pltpu.VMEM((2, group, hk), I32),
            pltpu.VMEM((2, group, d), F32),
            pltpu.VMEM((group, hk), F32),
            pltpu.VMEM((GATHER_BUFFERS, GATHER_ROWS, tiles, LANES), F32),
            pltpu.SemaphoreType.DMA((GATHER_BUFFERS,)),
            pltpu.SemaphoreType.DMA((2,)),
        ],
        compiler_params=pltpu.CompilerParams(needs_layout_passes=False),
        name="peer_dots",
    )(xn, idx, table, after)


def _peer_mix(w, idx, table):
    t, hk = w.shape
    tiles = table.shape[1]
    d = tiles * LANES
    per_worker, group, chunks = _sc_geometry(t, hk)

    def body(w_hbm, idx_hbm, table_hbm, out_hbm, idx_v, w_v, o_v, ring, sems, stage_sems, acc_v):
        wid = lax.axis_index("c") * SC_SUBCORES + lax.axis_index("s")

        def compute(par, g, c, slot):
            wv = w_v[par, g, pl.ds(c * GATHER_ROWS, GATHER_ROWS)]
            ws = [jnp.full((SC_LANES,), wv[r], F32) for r in range(GATHER_ROWS)]

            @plsc.parallel_loop(0, tiles * MIX_BLOCKS_PER_TILE)
            def _(cb):
                tile, lane0 = cb // MIX_BLOCKS_PER_TILE, (cb % MIX_BLOCKS_PER_TILE) * MIX_BLOCK_LANES
                sums = []
                for u in range(MIX_LANE_STEPS):
                    part = [None] * 4
                    for r in range(GATHER_ROWS):
                        term = ws[r] * ring[slot, r, tile, pl.ds(lane0 + u * SC_LANES, SC_LANES)]
                        part[r % 4] = term if part[r % 4] is None else part[r % 4] + term
                    sums.append((part[0] + part[1]) + (part[2] + part[3]))
                for u in range(MIX_LANE_STEPS):
                    col = cb * MIX_BLOCK_LANES + u * SC_LANES
                    if c == 0:
                        acc_v[pl.ds(col, SC_LANES)] = sums[u]
                    elif c < chunks - 1:
                        acc_v[pl.ds(col, SC_LANES)] = acc_v[pl.ds(col, SC_LANES)] + sums[u]
                    else:
                        o_v[g, pl.ds(col, SC_LANES)] = acc_v[pl.ds(col, SC_LANES)] + sums[u]

        _token_stream(wid * per_worker, per_worker, group, chunks, idx_hbm, w_hbm, table_hbm, out_hbm,
                      idx_v, w_v, o_v, ring, sems, stage_sems, compute)

    return pl.kernel(
        body,
        out_type=jax.ShapeDtypeStruct((t, d), F32),
        mesh=_sc_mesh(),
        scratch_types=[
            pltpu.VMEM((2, group, hk), I32),
            pltpu.VMEM((2, group, hk), F32),
            pltpu.VMEM((group, d), F32),
            pltpu.VMEM((GATHER_BUFFERS, GATHER_ROWS, tiles, LANES), F32),
            pltpu.SemaphoreType.DMA((GATHER_BUFFERS,)),
            pltpu.SemaphoreType.DMA((2,)),
            pltpu.VMEM((d,), F32),
        ],
        compiler_params=pltpu.CompilerParams(needs_layout_passes=False),
        name="peer_mix",
    )(w, idx, table)


def _gate_body(d_ref, gate_ref, w_ref):
    x = d_ref[...]
    sqrt_half = np.sqrt(0.5).astype(np.float32)
    w_ref[...] = gate_ref[...] * (0.5 * x * (1.0 + lax.erf(x * sqrt_half)))


def _gate(dots, gate):
    t, hk = dots.shape
    tm = min(t, LIGHT_ROW_TILE)
    assert t % tm == 0
    spec = pl.BlockSpec((tm, hk), lambda i: (i, 0))
    return pl.pallas_call(
        _gate_body, grid=(t // tm,), in_specs=[spec, spec], out_specs=spec,
        out_shape=jax.ShapeDtypeStruct((t, hk), F32),
        compiler_params=pltpu.CompilerParams(dimension_semantics=("parallel",)),
        name="gate",
    )(dots, gate)


def _residual_body(*refs, final):
    h_ref, p_ref, g_ref, y_ref = refs[-4:]
    x = h_ref[...] + p_ref[...]
    y_ref[...] = _rms(x, g_ref[...]) if final else x


def _residual(y_all, rows, y0, h2, h0, p, g, final):
    t, d = p.shape
    tm = min(t, LIGHT_ROW_TILE)
    assert t % tm == 0 and y0 % tm == 0 and h0 % tm == 0
    y_first, h_first = y0 // tm, h0 // tm
    first = y_all is None
    return pl.pallas_call(
        functools.partial(_residual_body, final=final),
        grid=(t // tm,),
        in_specs=([] if first else [pl.BlockSpec(memory_space=pl.ANY)]) + [
            pl.BlockSpec((tm, d), lambda i: (h_first + i, 0)),
            pl.BlockSpec((tm, d), lambda i: (i, 0)),
            _const_spec((1, d))],
        out_specs=pl.BlockSpec((tm, d), lambda i: (y_first + i, 0)),
        out_shape=jax.ShapeDtypeStruct((rows, d), F32),
        input_output_aliases={} if first else {0: 0},
        compiler_params=pltpu.CompilerParams(dimension_semantics=("parallel",)),
        name="residual",
    )(*(() if first else (y_all,)), h2, p, g.reshape(1, d))


SEQ_PARTS = 8


def _num_parts(n, l):
    per_part = (n // SEQ_PARTS) * l
    ok = n % SEQ_PARTS == 0 and per_part % (SC_WORKERS * SUBLANES) == 0
    return SEQ_PARTS if ok else 1


MIX_LAG = 1
FIRST_PIECE_TOKENS = 1024


def _token_pieces(q, tokens):
    quantum = SC_WORKERS * GROUP_TOKENS
    rest = tokens - FIRST_PIECE_TOKENS
    if q == 0 and rest > 0 and FIRST_PIECE_TOKENS % quantum == 0 and rest % quantum == 0:
        return [(0, FIRST_PIECE_TOKENS), (FIRST_PIECE_TOKENS, rest)]
    return [(0, tokens)]


def _trunk(x, hist_a, hist_b, norm_mix_g, w_in, conv_a_w, conv_b_w, conv_b_b, conv_norm_g, conv_norm_b,
           out_norm_a_g, out_norm_b_g, w_out, norm_ffn_g, w_q, sub_keys, u_tiles, v_tiles, final_norm_g):
    n, l, d = x.shape
    depth = len(w_in)
    parts = _num_parts(n, l)
    seqs = n // parts
    new_a, new_b = [], []
    for layer in range(depth):
        last = layer == depth - 1
        y_all = None
        nas, nbs, mixed = [], [], []
        for q in range(parts):
            h, na, nb = _mixer(x, q * seqs, seqs, hist_a[layer], hist_b[layer], norm_mix_g[layer], w_in[layer],
                               conv_a_w[layer], conv_b_w[layer], conv_b_b[layer], conv_norm_g[layer],
                               conv_norm_b[layer], out_norm_a_g[layer], out_norm_b_g[layer], w_out[layer])
            h2 = h.reshape(seqs * l, d)
            for t0, tn in _token_pieces(q, seqs * l):
                xn, idx, gate = _route(h2, t0, tn, norm_ffn_g[layer], w_q[layer], sub_keys[layer])
                after = mixed[-MIX_LAG] if len(mixed) >= MIX_LAG else idx
                dots = _peer_dots(xn, idx, u_tiles[layer], after)
                w = _gate(dots, gate)
                p = _peer_mix(w, idx, v_tiles[layer])
                mixed.append(p)
                y_all = _residual(y_all, n * l, q * seqs * l + t0, h2, t0, p,
                                  final_norm_g if last else jnp.ones((d,), F32), final=last)
            nas.append(na)
            nbs.append(nb)
        x = y_all.reshape(n, l, d)
        new_a.append(jnp.concatenate(nas, axis=0))
        new_b.append(jnp.concatenate(nbs, axis=0))
    return x, jnp.stack(new_a), jnp.stack(new_b)


def kernel(x_prompt, x_sample, cache_conv_a, cache_conv_b, norm_mix_g, w_in, conv_a_w, conv_b_w, conv_b_b,
           conv_norm_g, conv_norm_b, out_norm_a_g, out_norm_b_g, w_out, norm_ffn_g, w_q, sub_keys,
           u_experts, v_experts, final_norm_g):
    depth = w_in.shape[0]
    _, heads, _, n_keys, d_key = sub_keys.shape
    u_tiles = [_row_tiles(u_experts[layer]) for layer in range(depth)]
    v_tiles = [_row_tiles(v_experts[layer]) for layer in range(depth)]
    w_in_s = [_split(w_in[layer]) for layer in range(depth)]
    w_out_s = [_split(w_out[layer]) for layer in range(depth)]
    w_q_s = [_split(w_q[layer]) for layer in range(depth)]
    sk_s = [_split(sub_keys[layer].reshape(heads * 2 * n_keys, d_key)).reshape(2, heads * 2, n_keys, d_key)
            for layer in range(depth)]
    weights = (norm_mix_g, w_in_s, conv_a_w, conv_b_w, conv_b_b, conv_norm_g, conv_norm_b, out_norm_a_g,
               out_norm_b_g, w_out_s, norm_ffn_g, w_q_s, sk_s, u_tiles, v_tiles, final_norm_g)
    n = x_prompt.shape[0]
    y_sample, conv_a_sample, conv_b_sample = _trunk(x_sample, cache_conv_a, cache_conv_b, *weights)
    zeros_a = jnp.zeros((depth, n) + cache_conv_a.shape[2:], x_prompt.dtype)
    zeros_b = jnp.zeros((depth, n) + cache_conv_b.shape[2:], x_prompt.dtype)
    y_prompt, conv_a_prompt, conv_b_prompt = _trunk(x_prompt, zeros_a, zeros_b, *weights)
    return (y_prompt, y_sample, conv_a_prompt, conv_b_prompt, conv_a_sample, conv_b_sample)
```
